```python
import math
import jax, jax.numpy as jnp
from jax import lax
import numpy as np

D_MODEL = 1024
BATCH = 2
SEQ = 8192
DEPTH = 2
DEC_BATCH = 32
DEC_SEQ = 1
PAST_LEN = 16384
PAGE_SIZE = 128

HEAD_DIM = 64
H_SB = 6
H_MOBA = 5
H_FOX = 5
W_SB = H_SB * HEAD_DIM
W_MOBA = H_MOBA * HEAD_DIM
W_FOX = H_FOX * HEAD_DIM
N_BRANCH = 3
Q_BLOCK = 128
MOBA_BLOCK = 256
MOBA_TOPK = 3
ROPE_DIM = HEAD_DIM // 4
ROPE_THETA = 500000.0
D_FF = 4 * D_MODEL
DN_ALPHA = (2 * DEPTH) ** 0.25
DN_BETA = (8 * DEPTH) ** -0.25
LN_EPS = 1e-5
IN_COLS = 3 * (W_SB + W_MOBA + W_FOX) + H_FOX + N_BRANCH * D_MODEL

kernel_name = 'hybrid_sb_moba_fox_decoder_step'


def _split_points():
    sizes = (W_SB,) * 3 + (W_MOBA,) * 3 + (W_FOX,) * 3 + (H_FOX,)
    return [int(s) for s in np.cumsum(sizes)]


def layer_norm(x, g, b):
    xf = x.astype(jnp.float32)
    mu = jnp.mean(xf, axis=-1, keepdims=True)
    var = jnp.mean(jnp.square(xf - mu), axis=-1, keepdims=True)
    return ((xf - mu) * lax.rsqrt(var + LN_EPS) * g + b).astype(x.dtype)


def partial_rope(x, pos):
    half = ROPE_DIM // 2
    inv_freq = ROPE_THETA ** (-jnp.arange(half, dtype=jnp.float32) / half)
    ang = pos.astype(jnp.float32)[:, None] * inv_freq[None, :]
    cos = jnp.cos(ang)[None, :, None, :]
    sin = jnp.sin(ang)[None, :, None, :]
    xr = x[..., :ROPE_DIM].astype(jnp.float32)
    x1, x2 = xr[..., :half], xr[..., half:]
    rot = jnp.concatenate([x1 * cos - x2 * sin, x2 * cos + x1 * sin], axis=-1).astype(x.dtype)
    return jnp.concatenate([rot, x[..., ROPE_DIM:]], axis=-1)


def sweep_queries(block_fn, q, qpos):
    B, T, H, d = q.shape
    blk = Q_BLOCK if T % Q_BLOCK == 0 else T
    nb = T // blk
    qb = jnp.moveaxis(q.reshape(B, nb, blk, H, d), 1, 0)
    pb = qpos.reshape(nb, blk)
    out = lax.map(lambda a: block_fn(a[0], a[1]), (qb, pb))
    return jnp.moveaxis(out, 0, 1).reshape(B, T, H, out.shape[-1])


def stick_breaking_attention(q, k, v, qpos):
    kpos = jnp.arange(k.shape[1], dtype=jnp.int32)
    scale = HEAD_DIM ** -0.5

    def block(qb, qp):
        z = jnp.einsum('bqhd,bkhd->bhqk', qb, k).astype(jnp.float32) * scale
        strict = kpos[None, :] < qp[:, None]
        log_1m = jnp.where(strict, jax.nn.log_sigmoid(-z), 0.0)
        after = lax.cumsum(log_1m, axis=3, reverse=True) - log_1m
        w = jnp.where(strict, jnp.exp(jax.nn.log_sigmoid(z) + after), 0.0)
        return jnp.einsum('bhqk,bkhd->bqhd', w.astype(v.dtype), v)

    return sweep_queries(block, q, qpos)


def forgetting_attention(q, k, v, logf, qpos):
    kpos = jnp.arange(k.shape[1], dtype=jnp.int32)
    scale = HEAD_DIM ** -0.5
    c = jnp.cumsum(logf.astype(jnp.float32), axis=1).transpose(0, 2, 1)

    def block(qb, qp):
        s = jnp.einsum('bqhd,bkhd->bhqk', qb, k).astype(jnp.float32) * scale
        cq = jnp.take(c, qp, axis=2)
        s = s + cq[..., None] - c[:, :, None, :]
        s = jnp.where(kpos[None, :] <= qp[:, None], s, -jnp.inf)
        p = jax.nn.softmax(s, axis=-1)
        return jnp.einsum('bhqk,bkhd->bqhd', p.astype(v.dtype), v)

    return sweep_queries(block, q, qpos)


def moba_attention(q, k, v, qpos):
    B, Tk, H, d = k.shape
    scale = HEAD_DIM ** -0.5
    nb = -(-Tk // MOBA_BLOCK)
    pad = nb * MOBA_BLOCK - Tk
    kb = jnp.pad(k, ((0, 0), (0, pad), (0, 0), (0, 0))).reshape(B, nb, MOBA_BLOCK, H, d).transpose(0, 3, 1, 2, 4)
    vb = jnp.pad(v, ((0, 0), (0, pad), (0, 0), (0, 0))).reshape(B, nb, MOBA_BLOCK, H, d).transpose(0, 3, 1, 2, 4)
    kmean = jnp.mean(kb.astype(jnp.float32), axis=3)
    topk = min(MOBA_TOPK, nb)
    bi = jnp.arange(B)[:, None, None, None]
    hi = jnp.arange(H)[None, :, None, None]
    offs = jnp.arange(MOBA_BLOCK, dtype=jnp.int32)

    def block(qb, qp):
        own = qp // MOBA_BLOCK
        s = jnp.einsum('bqhd,bhnd->bhqn', qb.astype(jnp.float32), kmean)
        cand = jnp.arange(nb)[None, :] < own[:, None]
        s = jnp.where(cand, s, -jnp.inf)
        _, top_i = lax.top_k(s, topk)
        own_b = jnp.broadcast_to(own[None, None, :, None], top_i.shape[:3] + (1,)).astype(top_i.dtype)
        idx = jnp.concatenate([top_i, own_b], axis=-1)
        sel_ok = jnp.concatenate([top_i < own[None, None, :, None], jnp.ones(own_b.shape, bool)], axis=-1)
        kg = kb[bi, hi, idx]
        vg = vb[bi, hi, idx]
        kp = idx[..., None] * MOBA_BLOCK + offs
        mask = sel_ok[..., None] & (kp <= qp[None, None, :, None, None])
        logit = jnp.einsum('bqhd,bhqnkd->bhqnk', qb, kg).astype(jnp.float32) * scale
        logit = jnp.where(mask, logit, -jnp.inf)
        shp = logit.shape
        p = jax.nn.softmax(logit.reshape(shp[0], shp[1], shp[2], -1), axis=-1).reshape(shp)
        return jnp.einsum('bhqnk,bhqnkd->bqhd', p.astype(vg.dtype), vg)

    return sweep_queries(block, q, qpos)


def decoder_layer(x, past, w_in, b_forget, w_br_sb, w_br_moba, w_br_fox, w_o,
                  ln1_g, ln1_b, w_up, w_down, ln2_g, ln2_b):
    B, T, _ = x.shape
    p_len = 0 if past is None else past[0].shape[1]
    qpos = p_len + jnp.arange(T, dtype=jnp.int32)
    h = x @ w_in
    (q_sb, k_sb, v_sb, q_mb, k_mb, v_mb, q_fx, k_fx, v_fx, f_logit, g_logit) = jnp.split(h, _split_points(), axis=-1)

    def heads(t, n):
        return t.reshape(B, T, n, HEAD_DIM)

    q_sb, k_sb, v_sb = heads(q_sb, H_SB), heads(k_sb, H_SB), heads(v_sb, H_SB)
    q_mb = partial_rope(heads(q_mb, H_MOBA), qpos)
    k_mb = partial_rope(heads(k_mb, H_MOBA), qpos)
    v_mb = heads(v_mb, H_MOBA)
    q_fx, k_fx, v_fx = heads(q_fx, H_FOX), heads(k_fx, H_FOX), heads(v_fx, H_FOX)
    logf = jax.nn.log_sigmoid((f_logit + b_forget).astype(jnp.float32))

    kv_sb = jnp.stack([k_sb, v_sb], axis=2)
    kv_mb = jnp.stack([k_mb, v_mb], axis=2)
    kv_fx = jnp.stack([k_fx, v_fx], axis=2)
    new_rows = (kv_sb, kv_mb, kv_fx, logf)
    logf_all = logf
    if past is not None:
        kv_sb = jnp.concatenate([past[0].astype(kv_sb.dtype), kv_sb], axis=1)
        kv_mb = jnp.concatenate([past[1].astype(kv_mb.dtype), kv_mb], axis=1)
        kv_fx = jnp.concatenate([past[2].astype(kv_fx.dtype), kv_fx], axis=1)
        logf_all = jnp.concatenate([past[3].astype(jnp.float32), logf], axis=1)

    o_sb = stick_breaking_attention(q_sb, kv_sb[:, :, 0], kv_sb[:, :, 1], qpos)
    o_mb = moba_attention(q_mb, kv_mb[:, :, 0], kv_mb[:, :, 1], qpos)
    o_fx = forgetting_attention(q_fx, kv_fx[:, :, 0], kv_fx[:, :, 1], logf_all, qpos)

    gates = jax.nn.sigmoid(g_logit.astype(jnp.float32)).astype(x.dtype).reshape(B, T, N_BRANCH, D_MODEL)
    merged = (gates[:, :, 0] * (o_sb.reshape(B, T, W_SB) @ w_br_sb)
              + gates[:, :, 1] * (o_mb.reshape(B, T, W_MOBA) @ w_br_moba)
              + gates[:, :, 2] * (o_fx.reshape(B, T, W_FOX) @ w_br_fox))
    x = layer_norm(DN_ALPHA * x + merged @ w_o, ln1_g, ln1_b)
    x = layer_norm(DN_ALPHA * x + jnp.square(jax.nn.relu(x @ w_up)) @ w_down, ln2_g, ln2_b)
    return x, new_rows


def setup_inputs(seed: int = 0) -> dict:
    key = jax.random.key(seed)
    ks = jax.random.split(key, 24)
    n_pages = PAST_LEN // PAGE_SIZE
    n_pool = (DEC_BATCH * n_pages * 5) // 4

    def nrm(k, shape, scale):
        return jax.random.normal(k, shape, jnp.float32) * scale

    page_table = jax.random.permutation(ks[6], n_pool)[:DEC_BATCH * n_pages]
    page_table = page_table.reshape(DEC_BATCH, n_pages).astype(jnp.int32)
    return {
        'x_prompt': nrm(ks[0], (BATCH, SEQ, D_MODEL), 1.0),
        'x_sample': nrm(ks[1], (DEC_BATCH, DEC_SEQ, D_MODEL), 1.0),
        'cache_kv_sb': nrm(ks[2], (DEPTH, n_pool, PAGE_SIZE, 2, H_SB, HEAD_DIM), 1.0),
        'cache_kv_moba': nrm(ks[3], (DEPTH, n_pool, PAGE_SIZE, 2, H_MOBA, HEAD_DIM), 1.0),
        'cache_kv_fox': nrm(ks[4], (DEPTH, n_pool, PAGE_SIZE, 2, H_FOX, HEAD_DIM), 1.0),
        'cache_logf_fox': jax.nn.log_sigmoid(nrm(ks[5], (DEPTH, n_pool, PAGE_SIZE, H_FOX), 1.0) + 2.0),
        'page_table': page_table,
        'w_in': nrm(ks[7], (DEPTH, D_MODEL, IN_COLS), D_MODEL ** -0.5),
        'b_forget': 2.0 + nrm(ks[8], (DEPTH, H_FOX), 0.5),
        'w_br_sb': nrm(ks[9], (DEPTH, W_SB, D_MODEL), W_SB ** -0.5 * DN_BETA),
        'w_br_moba': nrm(ks[10], (DEPTH, W_MOBA, D_MODEL), W_MOBA ** -0.5 * DN_BETA),
        'w_br_fox': nrm(ks[11], (DEPTH, W_FOX, D_MODEL), W_FOX ** -0.5 * DN_BETA),
        'w_o': nrm(ks[12], (DEPTH, D_MODEL, D_MODEL), D_MODEL ** -0.5 * DN_BETA),
        'ln1_g': 1.0 + nrm(ks[13], (DEPTH, D_MODEL), 0.02),
        'ln1_b': nrm(ks[14], (DEPTH, D_MODEL), 0.02),
        'w_up': nrm(ks[15], (DEPTH, D_MODEL, D_FF), D_MODEL ** -0.5),
        'w_down': nrm(ks[16], (DEPTH, D_FF, D_MODEL), D_FF ** -0.5 * DN_BETA),
        'ln2_g': 1.0 + nrm(ks[17], (DEPTH, D_MODEL), 0.02),
        'ln2_b': nrm(ks[18], (DEPTH, D_MODEL), 0.02),
    }


def reference(x_prompt, x_sample, cache_kv_sb, cache_kv_moba, cache_kv_fox, cache_logf_fox, page_table,
              w_in, b_forget, w_br_sb, w_br_moba, w_br_fox, w_o, ln1_g, ln1_b, w_up, w_down, ln2_g, ln2_b):
    n_pages = page_table.shape[1]
    past_len = n_pages * PAGE_SIZE

    def gather(pool, l):
        g = pool[l, page_table]
        return g.reshape((g.shape[0], past_len) + g.shape[3:])

    yp, ys = x_prompt, x_sample
    rows_p, rows_s = [], []
    for l in range(DEPTH):
        params = (w_in[l], b_forget[l], w_br_sb[l], w_br_moba[l], w_br_fox[l], w_o[l],
                  ln1_g[l], ln1_b[l], w_up[l], w_down[l], ln2_g[l], ln2_b[l])
        yp, rp = decoder_layer(yp, None, *params)
        past = (gather(cache_kv_sb, l), gather(cache_kv_moba, l), gather(cache_kv_fox, l), gather(cache_logf_fox, l))
        ys, rs = decoder_layer(ys, past, *params)
        rows_p.append(rp)
        rows_s.append(rs)

    kv_sb_prompt = jnp.stack([r[0] for r in rows_p])
    kv_moba_prompt = jnp.stack([r[1] for r in rows_p])
    kv_fox_prompt = jnp.stack([r[2] for r in rows_p])
    logf_fox_prompt = jnp.stack([r[3] for r in rows_p])
    kv_sb_sample = jnp.stack([r[0] for r in rows_s])
    kv_moba_sample = jnp.stack([r[1] for r in rows_s])
    kv_fox_sample = jnp.stack([r[2] for r in rows_s])
    logf_fox_sample = jnp.stack([r[3] for r in rows_s])
    return (yp, ys, kv_sb_prompt, kv_moba_prompt, kv_fox_prompt, logf_fox_prompt,
            kv_sb_sample, kv_moba_sample, kv_fox_sample, logf_fox_sample)
```

```python
import functools

import jax
import jax.numpy as jnp
from jax import lax
from jax.experimental import pallas as pl
from jax.experimental.pallas import tpu as pltpu

bf16 = jnp.bfloat16
f32 = jnp.float32

D_MODEL = 1024
HEAD_DIM = 64
H_SB, H_MB, H_FX = 6, 5, 5
H_ALL = H_SB + H_MB + H_FX
W_SB, W_MB, W_FX = H_SB * HEAD_DIM, H_MB * HEAD_DIM, H_FX * HEAD_DIM
N_BRANCH = 3
MOBA_BLOCK = 256
MOBA_TOPK = 3
ROPE_DIM = HEAD_DIM // 4
ROPE_HALF = ROPE_DIM // 2
ROPE_THETA = 500000.0
D_FF = 4 * D_MODEL
DEPTH = 2
DN_ALPHA = (2 * DEPTH) ** 0.25
LN_EPS = 1e-5
PAGE = 128
SUBLANES = 8
LANES = 128
VMEM_LIMIT = 56 << 20
NEG = -1e30
EXP_ZERO_BELOW = -104.0
TM_INPROJ = 256
TM_POST = 256
FF_CHUNK = 1024
PAGES_PER_STEP = 8
PAGES_PER_BLOCK = MOBA_BLOCK // PAGE

NT_DIMS = (((1,), (1,)), ((), ()))


def _dot(a, b):
    return jnp.dot(a, b, preferred_element_type=f32)


def _dot_nt(a, b):
    return lax.dot_general(a, b, NT_DIMS, preferred_element_type=f32)


def _split2(x):
    hi = x.astype(bf16)
    lo = (x - hi.astype(f32)).astype(bf16)
    return hi, lo


def _split3(x):
    hi = x.astype(bf16)
    r = x - hi.astype(f32)
    mid = r.astype(bf16)
    lo = (r - mid.astype(f32)).astype(bf16)
    return hi, mid, lo


def _dot3(x, m):
    hi, mid, lo = _split3(x)
    return _dot(hi, m) + _dot(mid, m) + _dot(lo, m)


def _lane_sum_rows(a):
    ones = jnp.ones((SUBLANES, a.shape[1]), bf16)
    hi, mid, lo = _split3(a)
    return _dot_nt(ones, hi) + _dot_nt(ones, mid) + _dot_nt(ones, lo)


def _softplus(z):
    return jnp.maximum(z, 0.0) + jnp.log1p(jnp.exp(-jnp.abs(z)))


def _params(n_axes):
    return pltpu.CompilerParams(dimension_semantics=("arbitrary",) * n_axes,
                                vmem_limit_bytes=VMEM_LIMIT)


def _resident(shape):
    nd = len(shape)
    return pl.BlockSpec(shape, lambda *_: (0,) * nd, pipeline_mode=pl.Buffered(1))


def _inproj_kernel(x_ref, wq_ref, wkv_ref, wf_ref, wg_ref, bf_ref, cq_ref, sq_ref, cos_ref, sin_ref,
                   q_ref, kvsb_ref, kvmb_ref, kvfx_ref, logf_ref, c_ref, gate_ref, carry_ref, *, tm):
    t = pl.program_id(1)
    x = x_ref[0]
    xb, xlo = _split2(x)

    qf = _dot(xb, wq_ref[...])
    q_mb = (qf[:, W_SB:W_SB + W_MB] * cq_ref[...]
            + qf[:, W_SB + W_MB + W_FX:] * sq_ref[...])
    for h in range(H_SB):
        q_ref[0, h] = qf[:, h * HEAD_DIM:(h + 1) * HEAD_DIM]
    for h in range(H_MB):
        q_ref[0, H_SB + h] = q_mb[:, h * HEAD_DIM:(h + 1) * HEAD_DIM]
    for h in range(H_FX):
        o = W_SB + W_MB + h * HEAD_DIM
        q_ref[0, H_SB + H_MB + h] = qf[:, o:o + HEAD_DIM]

    kvt = _dot_nt(wkv_ref[...], xb)
    for h in range(2 * H_SB):
        kvsb_ref[0, h // H_SB, h % H_SB] = kvt[h * HEAD_DIM:(h + 1) * HEAD_DIM]
    base = 2 * W_SB
    cos = cos_ref[...]
    sin = sin_ref[...]
    for h in range(H_MB):
        r = base + h * HEAD_DIM
        x1 = kvt[r:r + ROPE_HALF]
        x2 = kvt[r + ROPE_HALF:r + ROPE_DIM]
        kvmb_ref[0, 0, h, 0:ROPE_HALF] = x1 * cos - x2 * sin
        kvmb_ref[0, 0, h, ROPE_HALF:ROPE_DIM] = x2 * cos + x1 * sin
        kvmb_ref[0, 0, h, ROPE_DIM:] = kvt[r + ROPE_DIM:r + HEAD_DIM]
        rv = base + W_MB + h * HEAD_DIM
        kvmb_ref[0, 1, h] = kvt[rv:rv + HEAD_DIM]
    base = 2 * W_SB + 2 * W_MB
    for h in range(2 * H_FX):
        kvfx_ref[0, h // H_FX, h % H_FX] = kvt[base + h * HEAD_DIM:base + (h + 1) * HEAD_DIM]

    r1 = _dot_nt(wf_ref[...], xb)
    r2 = _dot_nt(wf_ref[0:SUBLANES], xlo)
    fl = r1[0:SUBLANES] + r1[SUBLANES:] + r2 + bf_ref[...]
    logf = jnp.minimum(fl, 0.0) - jnp.log1p(jnp.exp(-jnp.abs(fl)))
    logf_ref[0] = logf

    @pl.when(t == 0)
    def _():
        carry_ref[...] = jnp.zeros_like(carry_ref)
    row = lax.broadcasted_iota(jnp.int32, (tm, tm), 0)
    col = lax.broadcasted_iota(jnp.int32, (tm, tm), 1)
    tri = (row <= col).astype(bf16)
    c = _dot3(logf, tri) + carry_ref[...]
    c_ref[0] = c
    carry_ref[...] = c[:, tm - 1:tm]

    g = _dot(xb, wg_ref[...])
    gate_ref[0] = (1.0 / (1.0 + jnp.exp(-g))).astype(gate_ref.dtype)


def _inproj(x, w, rope, tm):
    B, T, D = x.shape
    wq, wkv, wf, wg, bfc = w
    cq, sq, cos_t, sin_t = rope
    tok = lambda b, t: (b, t, 0)
    out_shape = (
        jax.ShapeDtypeStruct((B, H_ALL, T, HEAD_DIM), f32),
        jax.ShapeDtypeStruct((B, 2, H_SB, HEAD_DIM, T), f32),
        jax.ShapeDtypeStruct((B, 2, H_MB, HEAD_DIM, T), f32),
        jax.ShapeDtypeStruct((B, 2, H_FX, HEAD_DIM, T), f32),
        jax.ShapeDtypeStruct((B, SUBLANES, T), f32),
        jax.ShapeDtypeStruct((B, SUBLANES, T), f32),
        jax.ShapeDtypeStruct((B, T, N_BRANCH * D), bf16),
    )
    kv_spec = lambda H: pl.BlockSpec((1, 2, H, HEAD_DIM, tm), lambda b, t: (b, 0, 0, 0, t))
    row8 = pl.BlockSpec((1, SUBLANES, tm), lambda b, t: (b, 0, t))
    return pl.pallas_call(
        functools.partial(_inproj_kernel, tm=tm),
        grid=(B, T // tm),
        in_specs=[pl.BlockSpec((1, tm, D), tok),
                  _resident(wq.shape), _resident(wkv.shape), _resident(wf.shape), _resident(wg.shape),
                  _resident(bfc.shape),
                  pl.BlockSpec((tm, W_MB), lambda b, t: (t, 0)),
                  pl.BlockSpec((tm, W_MB), lambda b, t: (t, 0)),
                  pl.BlockSpec((ROPE_HALF, tm), lambda b, t: (0, t)),
                  pl.BlockSpec((ROPE_HALF, tm), lambda b, t: (0, t))],
        out_specs=(pl.BlockSpec((1, H_ALL, tm, HEAD_DIM), lambda b, t: (b, 0, t, 0)),
                   kv_spec(H_SB), kv_spec(H_MB), kv_spec(H_FX), row8, row8,
                   pl.BlockSpec((1, tm, N_BRANCH * D), tok)),
        out_shape=out_shape,
        scratch_shapes=[pltpu.VMEM((SUBLANES, 1), f32)],
        compiler_params=_params(2),
        name="inproj",
    )(x, wq, wkv, wf, wg, bfc, cq, sq, cos_t, sin_t)


def _tile_iotas(tq):
    row = lax.broadcasted_iota(jnp.int32, (tq, tq), 0)
    col = lax.broadcasted_iota(jnp.int32, (tq, tq), 1)
    return row, col


def _kv_block(k_ref, v_ref, j, tq):
    start = pl.multiple_of(j * tq, tq)
    kt = k_ref[0, 0, 0, :, pl.ds(start, tq)].astype(bf16)
    vt = v_ref[0, 0, 0, :, pl.ds(start, tq)].astype(bf16)
    return kt, vt


def _sb_kernel(q_ref, k_ref, v_ref, o_ref, acc_ref, carry_ref, *, tq):
    i = pl.program_id(2)
    qb = q_ref[0, 0].astype(bf16)
    row, col = _tile_iotas(tq)
    later = (row > col).astype(bf16)
    strict = col < row

    def block(j, diag):
        kt, vt = _kv_block(k_ref, v_ref, j, tq)
        z = _dot(qb, kt)
        l1m = -_softplus(z)
        if diag:
            l1m = jnp.where(strict, l1m, 0.0)
        hi, lo = _split2(l1m)
        after = _dot(hi, later) + _dot(lo, later) + carry_ref[...]
        w = jnp.exp(z + l1m + after)
        if diag:
            w = jnp.where(strict, w, 0.0)
        acc_ref[...] += _dot_nt(w.astype(bf16), vt)
        carry_ref[...] += jnp.sum(l1m, axis=1, keepdims=True)

    acc_ref[...] = jnp.zeros_like(acc_ref)
    carry_ref[...] = jnp.zeros_like(carry_ref)
    block(i, True)

    def cond(s):
        j, top = s
        return jnp.logical_and(j >= 0, top > EXP_ZERO_BELOW)

    def body(s):
        j, _ = s
        block(j, False)
        return j - 1, jnp.max(carry_ref[...])

    lax.while_loop(cond, body, (i - 1, jnp.max(carry_ref[...])))
    o_ref[0, 0] = acc_ref[...].astype(o_ref.dtype)


def _softmax_step(s, vt, m_ref, l_ref, acc_ref):
    m_old = m_ref[...]
    m_new = jnp.maximum(m_old, jnp.max(s, axis=1, keepdims=True))
    alpha = jnp.exp(m_old - m_new)
    p = jnp.exp(s - m_new)
    l_ref[...] = alpha * l_ref[...] + jnp.sum(p, axis=1, keepdims=True)
    acc_ref[...] = alpha * acc_ref[...] + _dot_nt(p.astype(bf16), vt)
    m_ref[...] = m_new


def _softmax_init(m_ref, l_ref, acc_ref):
    m_ref[...] = jnp.full_like(m_ref, NEG)
    l_ref[...] = jnp.zeros_like(l_ref)
    acc_ref[...] = jnp.zeros_like(acc_ref)


def _fox_kernel(q_ref, k_ref, v_ref, c_ref, o_ref, m_ref, l_ref, acc_ref, *, tq):
    h = pl.program_id(1)
    i = pl.program_id(2)
    qb = q_ref[0, 0].astype(bf16)
    row, col = _tile_iotas(tq)
    causal = col <= row

    def c_row(j):
        start = pl.multiple_of(j * tq, tq)
        return c_ref[0, pl.ds(h, 1), pl.ds(start, tq)]

    c_base = c_row(i)[:, 0:1]

    def block(j, diag):
        kt, vt = _kv_block(k_ref, v_ref, j, tq)
        s = _dot(qb, kt) - (c_row(j) - c_base)
        if diag:
            s = jnp.where(causal, s, NEG)
        _softmax_step(s, vt, m_ref, l_ref, acc_ref)

    _softmax_init(m_ref, l_ref, acc_ref)
    block(i, True)

    def body(n, _):
        block(i - 1 - n, False)
        return 0

    lax.fori_loop(0, i, body, 0)
    o_ref[0, 0] = (acc_ref[...] / l_ref[...]).astype(o_ref.dtype)


def _top_blocks(gate, cand, lane):
    gate = jnp.where(cand, gate, NEG)
    chosen = jnp.zeros(gate.shape, f32)
    firsts = []
    for _ in range(MOBA_TOPK):
        top = jnp.max(gate, axis=1, keepdims=True)
        first = jnp.min(jnp.where(gate == top, lane, LANES), axis=1, keepdims=True)
        pick = lane == first
        chosen = jnp.where(jnp.logical_and(pick, cand), 1.0, chosen)
        gate = jnp.where(pick, -jnp.inf, gate)
        firsts.append(first)
    return chosen, firsts


def _moba_kernel(q_ref, k_ref, v_ref, o_ref, km_ref, m_ref, l_ref, acc_ref, *, tq, nb):
    i = pl.program_id(2)

    @pl.when(i == 0)
    def _():
        lane = lax.broadcasted_iota(jnp.int32, (HEAD_DIM, LANES), 1)

        def body(j, km):
            start = pl.multiple_of(j * tq, tq)
            blk = k_ref[0, 0, 0, :, pl.ds(start, tq)]
            mean = jnp.sum(blk, axis=1, keepdims=True) * (1.0 / tq)
            return jnp.where(lane == j, mean, km)

        km_ref[...] = lax.fori_loop(0, nb, body, jnp.zeros((HEAD_DIM, LANES), f32))

    qf = q_ref[0, 0]
    qb = qf.astype(bf16)
    qh, ql = _split2(qf)
    kh, kl = _split2(km_ref[...])
    gate = _dot(qh, kh) + _dot(ql, kh) + _dot(qh, kl)
    lane = lax.broadcasted_iota(jnp.int32, (tq, LANES), 1)
    chosen, _ = _top_blocks(gate, lane < i, lane)

    row, col = _tile_iotas(tq)
    causal = col <= row

    _softmax_init(m_ref, l_ref, acc_ref)
    kt, vt = _kv_block(k_ref, v_ref, i, tq)
    _softmax_step(jnp.where(causal, _dot(qb, kt), NEG), vt, m_ref, l_ref, acc_ref)

    def body(j, _):
        sel = jnp.max(jnp.where(lane == j, chosen, 0.0), axis=1, keepdims=True)

        @pl.when(jnp.max(sel) > 0.0)
        def _():
            kt, vt = _kv_block(k_ref, v_ref, j, tq)
            s = jnp.where(sel > 0.0, _dot(qb, kt), NEG)
            _softmax_step(s, vt, m_ref, l_ref, acc_ref)

        return 0

    lax.fori_loop(0, i, body, 0)
    o_ref[0, 0] = (acc_ref[...] / l_ref[...]).astype(o_ref.dtype)


def _prompt_attention(kernel, name, q, kvt, head_off, scratch, extra=(), extra_specs=()):
    B, _, T, hd = q.shape
    H = kvt.shape[2]
    tq = MOBA_BLOCK
    return pl.pallas_call(
        kernel,
        grid=(B, H, T // tq),
        in_specs=[pl.BlockSpec((1, 1, tq, hd), lambda b, h, i: (b, h + head_off, i, 0)),
                  pl.BlockSpec((1, 1, 1, hd, T), lambda b, h, i: (b, 0, h, 0, 0)),
                  pl.BlockSpec((1, 1, 1, hd, T), lambda b, h, i: (b, 1, h, 0, 0)),
                  *extra_specs],
        out_specs=pl.BlockSpec((1, 1, tq, hd), lambda b, h, i: (b, h, i, 0)),
        out_shape=jax.ShapeDtypeStruct((B, H, T, hd), bf16),
        scratch_shapes=scratch,
        compiler_params=_params(3),
        name=name,
    )(q, kvt, kvt, *extra)


def _softmax_scratch(tq):
    return [pltpu.VMEM((tq, 1), f32), pltpu.VMEM((tq, 1), f32), pltpu.VMEM((tq, HEAD_DIM), f32)]


def _sb_prompt(q, kvt):
    tq = MOBA_BLOCK
    return _prompt_attention(functools.partial(_sb_kernel, tq=tq), "sb_prompt", q, kvt, 0,
                             [pltpu.VMEM((tq, HEAD_DIM), f32), pltpu.VMEM((tq, 1), f32)])


def _moba_prompt(q, kvt):
    tq = MOBA_BLOCK
    T = q.shape[2]
    return _prompt_attention(functools.partial(_moba_kernel, tq=tq, nb=T // tq), "moba_prompt", q, kvt, H_SB,
                             [pltpu.VMEM((HEAD_DIM, LANES), f32)] + _softmax_scratch(tq))


def _fox_prompt(q, kvt, c):
    tq = MOBA_BLOCK
    T = q.shape[2]
    return _prompt_attention(functools.partial(_fox_kernel, tq=tq), "fox_prompt", q, kvt, H_SB + H_MB,
                             _softmax_scratch(tq), extra=(c,),
                             extra_specs=(pl.BlockSpec((1, SUBLANES, T), lambda b, h, i: (b, 0, 0)),))


def _layer_norm(x, g, b):
    mu = jnp.mean(x, axis=-1, keepdims=True)
    xc = x - mu
    var = jnp.mean(xc * xc, axis=-1, keepdims=True)
    return xc * lax.rsqrt(var + LN_EPS) * g + b


def _post_kernel(x_ref, osb_ref, omb_ref, ofx_ref, gate_ref, wsb_ref, wmb_ref, wfx_ref, wo_ref, wup_ref, wdn_ref,
                 g1_ref, b1_ref, g2_ref, b2_ref, y_ref):
    x = x_ref[0]
    merged = None
    for br, (o_ref, w_ref) in enumerate(((osb_ref, wsb_ref), (omb_ref, wmb_ref), (ofx_ref, wfx_ref))):
        pb = None
        for h in range(o_ref.shape[1]):
            t = _dot(o_ref[0, h], w_ref[h])
            pb = t if pb is None else pb + t
        gb = gate_ref[0, :, br * D_MODEL:(br + 1) * D_MODEL].astype(f32)
        merged = gb * pb if merged is None else merged + gb * pb
    u = _dot(merged.astype(bf16), wo_ref[...])
    x1 = _layer_norm(DN_ALPHA * x + u, g1_ref[...], b1_ref[...])
    x1b = x1.astype(bf16)
    acc = None
    for f in range(D_FF // FF_CHUNK):
        hm = jnp.maximum(_dot(x1b, wup_ref[:, f * FF_CHUNK:(f + 1) * FF_CHUNK]), 0.0)
        t = _dot((hm * hm).astype(bf16), wdn_ref[f * FF_CHUNK:(f + 1) * FF_CHUNK, :])
        acc = t if acc is None else acc + t
    y_ref[0] = _layer_norm(DN_ALPHA * x1 + acc, g2_ref[...], b2_ref[...])


def _post(x, o_sb, o_mb, o_fx, gates, w, tm):
    B, T, D = x.shape
    tok = lambda b, t: (b, t, 0)
    o_spec = lambda H: pl.BlockSpec((1, H, tm, HEAD_DIM), lambda b, t: (b, 0, t, 0))
    return pl.pallas_call(
        _post_kernel,
        grid=(B, T // tm),
        in_specs=[pl.BlockSpec((1, tm, D), tok), o_spec(H_SB), o_spec(H_MB), o_spec(H_FX),
                  pl.BlockSpec((1, tm, N_BRANCH * D), tok)] + [_resident(a.shape) for a in w],
        out_specs=pl.BlockSpec((1, tm, D), tok),
        out_shape=jax.ShapeDtypeStruct((B, T, D), f32),
        compiler_params=_params(2),
        name="post",
    )(x, o_sb, o_mb, o_fx, gates, *w)


def _page_scores(z_ref, k_page, qb_ref, n_heads):
    for h in range(n_heads):
        z_ref[h:h + 1, :] = jnp.sum(k_page[h] * qb_ref[0, h], axis=0, keepdims=True)


def _later_matrix():
    row, col = _tile_iotas(PAGE)
    return (row > col).astype(bf16)


def _finish_heads(o_ref, acc_ref, n_heads, scale=None):
    o_ref[0] = jnp.zeros(o_ref.shape[1:], f32)
    for h in range(n_heads):
        r = _lane_sum_rows(acc_ref[h])[0:1]
        if scale is not None:
            r = r * scale[h:h + 1, 0:HEAD_DIM]
        o_ref[0, h:h + 1, :] = r


def _sb_dec_kernel(pt_ref, qb_ref, *refs, n_heads):
    pages = refs[:PAGES_PER_STEP]
    o_ref, z_ref, acc_ref, carry_ref = refs[PAGES_PER_STEP:]
    c = pl.program_id(1)

    @pl.when(c == 0)
    def _():
        z_ref[...] = jnp.zeros_like(z_ref)
        acc_ref[...] = jnp.zeros_like(acc_ref)
        carry_ref[...] = jnp.zeros_like(carry_ref)

    later = _later_matrix()
    for page in pages:
        @pl.when(jnp.max(carry_ref[0:n_heads]) > EXP_ZERO_BELOW)
        def _():
            _page_scores(z_ref, page.at[0], qb_ref, n_heads)
            z = z_ref[...]
            l1m = -_softplus(z)
            hi, lo = _split2(l1m)
            after = _dot(hi, later) + _dot(lo, later) + carry_ref[...]
            w = jnp.exp(z + l1m + after)
            for h in range(n_heads):
                acc_ref[h] += page[1, h] * w[h:h + 1, :]
            carry_ref[...] += jnp.sum(l1m, axis=1, keepdims=True)

    @pl.when(c == pl.num_programs(1) - 1)
    def _():
        _finish_heads(o_ref, acc_ref, n_heads)


def _fox_dec_kernel(pt_ref, qb_ref, kb_ref, vb_ref, lfn_ref, *refs, n_heads):
    pages = refs[:PAGES_PER_STEP]
    lf_pages = refs[PAGES_PER_STEP:2 * PAGES_PER_STEP]
    o_ref, z_ref, lf_ref, m_ref, l_ref, acc_ref, carry_ref = refs[2 * PAGES_PER_STEP:]
    c = pl.program_id(1)

    @pl.when(c == 0)
    def _():
        z_ref[...] = jnp.zeros_like(z_ref)
        lf_ref[...] = jnp.zeros_like(lf_ref)
        carry_ref[...] = jnp.zeros_like(carry_ref)
        _page_scores(z_ref, kb_ref.at[0], qb_ref, n_heads)
        m_ref[...] = z_ref[...]
        l_ref[...] = jnp.ones_like(l_ref)
        lane0 = lax.broadcasted_iota(jnp.int32, (HEAD_DIM, LANES), 1) == 0
        for h in range(n_heads):
            acc_ref[h] = jnp.where(lane0, vb_ref[0, h], 0.0)

    later = _later_matrix()
    for page, lf_page in zip(pages, lf_pages):
        _page_scores(z_ref, page.at[0], qb_ref, n_heads)
        for h in range(n_heads):
            lf_ref[h:h + 1, :] = lf_page[h]
        lf = lf_ref[...]
        s = z_ref[...] + _dot3(lf, later) + carry_ref[...] + lfn_ref[0]
        m_old = m_ref[...]
        m_new = jnp.maximum(m_old, jnp.max(s, axis=1, keepdims=True))
        alpha = jnp.exp(m_old - m_new)
        p = jnp.exp(s - m_new)
        l_ref[...] = alpha * l_ref[...] + jnp.sum(p, axis=1, keepdims=True)
        m_ref[...] = m_new
        for h in range(n_heads):
            acc_ref[h] = acc_ref[h] * alpha[h:h + 1, :] + page[1, h] * p[h:h + 1, :]
        carry_ref[...] += jnp.sum(lf, axis=1, keepdims=True)

    @pl.when(c == pl.num_programs(1) - 1)
    def _():
        _finish_heads(o_ref, acc_ref, n_heads, scale=1.0 / l_ref[...])


def _cache_view(cache):
    return jnp.transpose(cache, (0, 1, 3, 4, 5, 2))


def _bcast_spec(n_heads):
    return pl.BlockSpec((1, n_heads, HEAD_DIM, LANES), lambda b, c, pt: (b, 0, 0, 0))


def _page_specs(layer, n_heads, n_pages, newest_first, kv=None):
    specs = []
    for i in range(PAGES_PER_STEP):
        def page_of(b, c, pt, i=i):
            lp = c * PAGES_PER_STEP + i
            return pt[b, n_pages - 1 - lp] if newest_first else pt[b, lp]
        if kv is None:
            specs.append(pl.BlockSpec((None, None, 2, n_heads, HEAD_DIM, PAGE),
                                      lambda b, c, pt, f=page_of: (layer, f(b, c, pt), 0, 0, 0, 0)))
        else:
            specs.append(pl.BlockSpec((None, None, None, n_heads, HEAD_DIM, PAGE),
                                      lambda b, c, pt, f=page_of: (layer, f(b, c, pt), kv, 0, 0, 0)))
    return specs


def _dec_out(nseq):
    return (pl.BlockSpec((1, SUBLANES, HEAD_DIM), lambda b, c, pt: (b, 0, 0)),
            jax.ShapeDtypeStruct((nseq, SUBLANES, HEAD_DIM), f32))


def _sb_decode(cache_t, layer, page_table, qb):
    nseq, n_pages = page_table.shape
    n_heads = cache_t.shape[3]
    out_spec, out_shape = _dec_out(nseq)
    gs = pltpu.PrefetchScalarGridSpec(
        num_scalar_prefetch=1, grid=(nseq, n_pages // PAGES_PER_STEP),
        in_specs=[_bcast_spec(n_heads)] + _page_specs(layer, n_heads, n_pages, True),
        out_specs=out_spec,
        scratch_shapes=[pltpu.VMEM((SUBLANES, PAGE), f32), pltpu.VMEM((n_heads, HEAD_DIM, PAGE), f32),
                        pltpu.VMEM((SUBLANES, 1), f32)])
    return pl.pallas_call(functools.partial(_sb_dec_kernel, n_heads=n_heads), grid_spec=gs, out_shape=out_shape,
                          compiler_params=_params(2), name="sb_decode",
                          )(page_table, qb, *([cache_t] * PAGES_PER_STEP))


def _fox_decode(cache_t, logf_t, layer, page_table, qb, kb, vb, lfn):
    nseq, n_pages = page_table.shape
    n_heads = cache_t.shape[3]
    out_spec, out_shape = _dec_out(nseq)
    lf_specs = []
    for i in range(PAGES_PER_STEP):
        lf_specs.append(pl.BlockSpec(
            (None, n_heads, None, 1, PAGE),
            lambda b, c, pt, i=i: (layer, 0, pt[b, n_pages - 1 - (c * PAGES_PER_STEP + i)], 0, 0)))
    gs = pltpu.PrefetchScalarGridSpec(
        num_scalar_prefetch=1, grid=(nseq, n_pages // PAGES_PER_STEP),
        in_specs=[_bcast_spec(n_heads)] * 3 + [pl.BlockSpec((1, SUBLANES, LANES), lambda b, c, pt: (b, 0, 0))]
        + _page_specs(layer, n_heads, n_pages, True) + lf_specs,
        out_specs=out_spec,
        scratch_shapes=[pltpu.VMEM((SUBLANES, PAGE), f32), pltpu.VMEM((SUBLANES, PAGE), f32),
                        pltpu.VMEM((SUBLANES, PAGE), f32), pltpu.VMEM((SUBLANES, PAGE), f32),
                        pltpu.VMEM((n_heads, HEAD_DIM, PAGE), f32), pltpu.VMEM((SUBLANES, 1), f32)])
    return pl.pallas_call(functools.partial(_fox_dec_kernel, n_heads=n_heads), grid_spec=gs, out_shape=out_shape,
                          compiler_params=_params(2), name="fox_decode",
                          )(page_table, qb, kb, vb, lfn, *([cache_t] * PAGES_PER_STEP),
                            *([logf_t] * PAGES_PER_STEP))


def _mb_scores_kernel(pt_ref, qb_ref, kb_ref, *refs, n_heads):
    pages = refs[:PAGES_PER_STEP]
    z_out, bsum_ref, zself_ref, z_ref = refs[PAGES_PER_STEP:]
    c = pl.program_id(1)

    @pl.when(c == 0)
    def _():
        z_ref[...] = jnp.zeros_like(z_ref)
        bsum_ref[...] = jnp.zeros_like(bsum_ref)
        _page_scores(z_ref, kb_ref.at[0], qb_ref, n_heads)
        zself_ref[0] = z_ref[...]

    lane = lax.broadcasted_iota(jnp.int32, (SUBLANES, LANES), 1)
    for i, page in enumerate(pages):
        _page_scores(z_ref, page, qb_ref, n_heads)
        z = z_ref[...]
        z_out[0, :, i * PAGE:(i + 1) * PAGE] = z
        blk = (c * PAGES_PER_STEP + i) // PAGES_PER_BLOCK
        bsum_ref[0] += jnp.where(lane == blk, jnp.sum(z, axis=1, keepdims=True), 0.0)


def _mb_select_kernel(z_ref, bsum_ref, zself_ref, p_ref, pself_ref, idx_ref, *, n_blocks):
    gate = bsum_ref[0] * (1.0 / MOBA_BLOCK)
    lane = lax.broadcasted_iota(jnp.int32, (SUBLANES, LANES), 1)
    chosen, firsts = _top_blocks(gate, lane < n_blocks, lane)
    idx = jnp.zeros((SUBLANES, LANES), jnp.int32)
    for r, first in enumerate(firsts):
        idx = jnp.where(lane == r, first, idx)
    idx_ref[0] = idx
    n_keys = z_ref.shape[2]
    blk_of_key = lax.broadcasted_iota(jnp.int32, (SUBLANES, n_keys), 1) // MOBA_BLOCK
    picked = None
    for first in firsts:
        hit = jnp.logical_and(blk_of_key == first, first < n_blocks)
        picked = hit if picked is None else jnp.logical_or(picked, hit)
    s = jnp.where(picked, z_ref[0], NEG)
    zs = zself_ref[0]
    m = jnp.maximum(jnp.max(s, axis=1, keepdims=True), zs[:, 0:1])
    p = jnp.exp(s - m)
    ps = jnp.exp(zs - m)
    inv = 1.0 / (jnp.sum(p, axis=1, keepdims=True) + ps[:, 0:1])
    p_ref[0] = p * inv
    pself_ref[0] = ps * inv


def _mb_pv_kernel(pt_ref, idx_ref, v_ref, p_ref, pself_ref, vb_ref, o_ref, acc_ref):
    h = pl.program_id(1)
    k = pl.program_id(2)

    @pl.when(jnp.logical_and(h == 0, k == 0))
    def _():
        o_ref[...] = jnp.zeros_like(o_ref)

    @pl.when(k == 0)
    def _():
        lane0 = lax.broadcasted_iota(jnp.int32, (HEAD_DIM, LANES), 1) == 0
        acc_ref[...] = jnp.where(lane0, vb_ref[0, h] * pself_ref[0, pl.ds(h, 1), :], 0.0)

    acc_ref[...] += v_ref[...] * p_ref[0, pl.ds(h, 1), :]

    @pl.when(k == pl.num_programs(2) - 1)
    def _():
        o_ref[0, pl.ds(h, 1), :] = _lane_sum_rows(acc_ref[...])[0:1]


def _moba_decode(cache_t, layer, page_table, qb, kb, vb):
    nseq, n_pages = page_table.shape
    n_heads = cache_t.shape[3]
    n_keys = n_pages * PAGE
    n_blocks = n_keys // MOBA_BLOCK
    row8 = pl.BlockSpec((1, SUBLANES, LANES), lambda b, c, pt: (b, 0, 0))
    gs = pltpu.PrefetchScalarGridSpec(
        num_scalar_prefetch=1, grid=(nseq, n_pages // PAGES_PER_STEP),
        in_specs=[_bcast_spec(n_heads)] * 2 + _page_specs(layer, n_heads, n_pages, False, kv=0),
        out_specs=(pl.BlockSpec((1, SUBLANES, PAGES_PER_STEP * PAGE), lambda b, c, pt: (b, 0, c)), row8, row8),
        scratch_shapes=[pltpu.VMEM((SUBLANES, PAGE), f32)])
    z, bsum, zself = pl.pallas_call(
        functools.partial(_mb_scores_kernel, n_heads=n_heads), grid_spec=gs,
        out_shape=(jax.ShapeDtypeStruct((nseq, SUBLANES, n_keys), f32),
                   jax.ShapeDtypeStruct((nseq, SUBLANES, LANES), f32),
                   jax.ShapeDtypeStruct((nseq, SUBLANES, LANES), f32)),
        compiler_params=_params(2), name="moba_decode_scores",
    )(page_table, qb, kb, *([cache_t] * PAGES_PER_STEP))

    keys = pl.BlockSpec((1, SUBLANES, n_keys), lambda b: (b, 0, 0))
    small = pl.BlockSpec((1, SUBLANES, LANES), lambda b: (b, 0, 0))
    p, pself, idx = pl.pallas_call(
        functools.partial(_mb_select_kernel, n_blocks=n_blocks),
        grid=(nseq,), in_specs=[keys, small, small], out_specs=(keys, small, small),
        out_shape=(jax.ShapeDtypeStruct((nseq, SUBLANES, n_keys), f32),
                   jax.ShapeDtypeStruct((nseq, SUBLANES, LANES), f32),
                   jax.ShapeDtypeStruct((nseq, SUBLANES, LANES), jnp.int32)),
        compiler_params=_params(1), name="moba_decode_select",
    )(z, bsum, zself)

    idx_flat = idx[:, :, :MOBA_TOPK].reshape(nseq, SUBLANES * MOBA_TOPK)

    def page_no(b, h, k, idx):
        return idx[b, h * MOBA_TOPK + k // PAGES_PER_BLOCK] * PAGES_PER_BLOCK + k % PAGES_PER_BLOCK

    gs = pltpu.PrefetchScalarGridSpec(
        num_scalar_prefetch=2, grid=(nseq, n_heads, MOBA_TOPK * PAGES_PER_BLOCK),
        in_specs=[pl.BlockSpec((None, None, None, None, HEAD_DIM, PAGE),
                               lambda b, h, k, pt, idx: (layer, pt[b, page_no(b, h, k, idx)], 1, h, 0, 0)),
                  pl.BlockSpec((1, SUBLANES, PAGE), lambda b, h, k, pt, idx: (b, 0, page_no(b, h, k, idx))),
                  pl.BlockSpec((1, SUBLANES, LANES), lambda b, h, k, pt, idx: (b, 0, 0)),
                  pl.BlockSpec((1, n_heads, HEAD_DIM, LANES), lambda b, h, k, pt, idx: (b, 0, 0, 0))],
        out_specs=pl.BlockSpec((1, SUBLANES, HEAD_DIM), lambda b, h, k, pt, idx: (b, 0, 0)),
        scratch_shapes=[pltpu.VMEM((HEAD_DIM, PAGE), f32)])
    return pl.pallas_call(
        _mb_pv_kernel, grid_spec=gs, out_shape=jax.ShapeDtypeStruct((nseq, SUBLANES, HEAD_DIM), f32),
        compiler_params=_params(3), name="moba_decode_pv",
    )(page_table, idx_flat, cache_t, p, pself, vb)


def _rope_tables(pos):
    inv_freq = ROPE_THETA ** (-jnp.arange(ROPE_HALF, dtype=f32) / ROPE_HALF)
    ang = pos.astype(f32)[:, None] * inv_freq[None, :]
    cos, sin = jnp.cos(ang), jnp.sin(ang)
    n = pos.shape[0]
    rest = HEAD_DIM - ROPE_DIM
    cq = jnp.tile(jnp.concatenate([cos, cos, jnp.ones((n, rest), f32)], axis=1), (1, H_MB))
    sq = jnp.tile(jnp.concatenate([sin, sin, jnp.zeros((n, rest), f32)], axis=1), (1, H_MB))
    return cq, sq, cos.T, sin.T


def _layer_weights(w_in, b_forget):
    scale = HEAD_DIM ** -0.5
    edges = [0]
    for wdt in (W_SB,) * 3 + (W_MB,) * 3 + (W_FX,) * 3 + (H_FX,):
        edges.append(edges[-1] + wdt)
    seg = [w_in[:, a:b] for a, b in zip(edges[:-1], edges[1:])]
    q_sb, k_sb, v_sb, q_mb, k_mb, v_mb, q_fx, k_fx, v_fx, w_f = seg
    w_g = w_in[:, edges[-1]:]
    qm = q_mb.reshape(D_MODEL, H_MB, HEAD_DIM)
    q_sw = jnp.concatenate([-qm[..., ROPE_HALF:ROPE_DIM], qm[..., :ROPE_HALF],
                            jnp.zeros((D_MODEL, H_MB, HEAD_DIM - ROPE_DIM), f32)], axis=-1).reshape(D_MODEL, W_MB)
    wq = (jnp.concatenate([q_sb, q_mb, q_fx, q_sw], axis=1) * scale).astype(bf16)
    wkv = jnp.concatenate([k_sb, v_sb, k_mb, v_mb, k_fx, v_fx], axis=1).T.astype(bf16)
    wf_t = jnp.pad(w_f, ((0, 0), (0, SUBLANES - H_FX))).T
    wf_hi = wf_t.astype(bf16)
    wf_lo = (wf_t - wf_hi.astype(f32)).astype(bf16)
    wf = jnp.concatenate([wf_hi, wf_lo], axis=0)
    bfc = jnp.pad(b_forget, (0, SUBLANES - H_FX)).reshape(SUBLANES, 1)
    return wq, wkv, wf, w_g.astype(bf16), bfc


def _post_weights(w_br_sb, w_br_mb, w_br_fx, w_o, ln1_g, ln1_b, w_up, w_down, ln2_g, ln2_b):
    heads = lambda w: w.reshape(-1, HEAD_DIM, D_MODEL).astype(bf16)
    vec = lambda v: v.reshape(1, D_MODEL)
    return (heads(w_br_sb), heads(w_br_mb), heads(w_br_fx), w_o.astype(bf16), w_up.astype(bf16),
            w_down.astype(bf16), vec(ln1_g), vec(ln1_b), vec(ln2_g), vec(ln2_b))


def _lane_bcast(a):
    return jnp.broadcast_to(a[..., None], a.shape + (LANES,))


def kernel(x_prompt, x_sample, cache_kv_sb, cache_kv_moba, cache_kv_fox, cache_logf_fox, page_table,
           w_in, b_forget, w_br_sb, w_br_moba, w_br_fox, w_o, ln1_g, ln1_b, w_up, w_down, ln2_g, ln2_b):
    B, T, D = x_prompt.shape
    nseq = x_sample.shape[0]
    n_pages = page_table.shape[1]
    past_len = n_pages * PAGE

    rope_p = _rope_tables(jnp.arange(T, dtype=jnp.int32))
    rope_s = _rope_tables(jnp.full((nseq,), past_len, jnp.int32))
    sb_t, mb_t, fx_t = _cache_view(cache_kv_sb), _cache_view(cache_kv_moba), _cache_view(cache_kv_fox)
    pool = cache_logf_fox.shape[1]
    logf_t = jnp.transpose(cache_logf_fox, (0, 3, 1, 2)).reshape(DEPTH, H_FX, pool, 1, PAGE)

    yp = x_prompt
    ys = x_sample.reshape(1, nseq, D)
    rows_p, rows_s = [], []
    for l in range(DEPTH):
        w1 = _layer_weights(w_in[l], b_forget[l])
        w2 = _post_weights(w_br_sb[l], w_br_moba[l], w_br_fox[l], w_o[l], ln1_g[l], ln1_b[l],
                           w_up[l], w_down[l], ln2_g[l], ln2_b[l])

        q, kvsb, kvmb, kvfx, logf, c, gates = _inproj(yp, w1, rope_p, TM_INPROJ)
        o_sb = _sb_prompt(q, kvsb)
        o_mb = _moba_prompt(q, kvmb)
        o_fx = _fox_prompt(q, kvfx, c)
        yp = _post(yp, o_sb, o_mb, o_fx, gates, w2, TM_POST)
        rows_p.append((kvsb, kvmb, kvfx, logf))

        q, kvsb, kvmb, kvfx, logf, _, gates = _inproj(ys, w1, rope_s, nseq)
        qd = _lane_bcast(jnp.transpose(q[0], (1, 0, 2)))
        new = lambda kvt, j: _lane_bcast(jnp.transpose(kvt[0, j], (2, 0, 1)))
        lfn = _lane_bcast(logf[0].T)
        o_sb = _sb_decode(sb_t, l, page_table, qd[:, :H_SB])
        o_mb = _moba_decode(mb_t, l, page_table, qd[:, H_SB:H_SB + H_MB], new(kvmb, 0), new(kvmb, 1))
        o_fx = _fox_decode(fx_t, logf_t, l, page_table, qd[:, H_SB + H_MB:], new(kvfx, 0), new(kvfx, 1), lfn)
        heads = lambda o, H: jnp.transpose(o[:, :H], (1, 0, 2))[None].astype(bf16)
        ys = _post(ys, heads(o_sb, H_SB), heads(o_mb, H_MB), heads(o_fx, H_FX), gates, w2, nseq)
        rows_s.append((kvsb, kvmb, kvfx, logf))

    def kv_out(rows, j):
        return jnp.transpose(jnp.stack([r[j] for r in rows]), (0, 1, 5, 2, 3, 4))

    def logf_out(rows):
        return jnp.transpose(jnp.stack([r[3] for r in rows])[:, :, :H_FX], (0, 1, 3, 2))

    def sample(a):
        return jnp.swapaxes(a, 1, 2)

    return (yp, ys.reshape(nseq, 1, D),
            kv_out(rows_p, 0), kv_out(rows_p, 1), kv_out(rows_p, 2), logf_out(rows_p),
            sample(kv_out(rows_s, 0)), sample(kv_out(rows_s, 1)), sample(kv_out(rows_s, 2)),
            sample(logf_out(rows_s)))
```

```python
import functools

import jax
import jax.numpy as jnp
from jax import lax
from jax.experimental import pallas as pl
from jax.experimental.pallas import tpu as pltpu

bf16 = jnp.bfloat16
f32 = jnp.float32

D_MODEL = 1024
HEAD_DIM = 64
H_SB, H_MB, H_FX = 6, 5, 5
H_ALL = H_SB + H_MB + H_FX
W_SB, W_MB, W_FX = H_SB * HEAD_DIM, H_MB * HEAD_DIM, H_FX * HEAD_DIM
W_ALL = W_SB + W_MB + W_FX
N_BRANCH = 3
MOBA_BLOCK = 256
MOBA_TOPK = 3
ROPE_DIM = HEAD_DIM // 4
ROPE_HALF = ROPE_DIM // 2
ROPE_THETA = 500000.0
D_FF = 4 * D_MODEL
DEPTH = 2
DN_ALPHA = (2 * DEPTH) ** 0.25
LN_EPS = 1e-5
PAGE = 128
SUBLANES = 8
LANES = 128
VMEM_LIMIT = 56 << 20
NEG = -1e30
EXP_ZERO_BELOW = -104.0
TM_INPROJ = 256
TM_POST = 256
FF_CHUNK = 1024
TQ = MOBA_BLOCK
FOX_CHUNK_BLOCKS = 2
MOBA_CHUNK_BLOCKS = 4
NORM_ROWS = 1024
PAGES_PER_STEP = 8
PAGES_PER_BLOCK = MOBA_BLOCK // PAGE

NT_DIMS = (((1,), (1,)), ((), ()))


def _dot(a, b):
    return jnp.dot(a, b, preferred_element_type=f32)


def _dot_nt(a, b):
    return lax.dot_general(a, b, NT_DIMS, preferred_element_type=f32)


def _split2(x):
    hi = x.astype(bf16)
    lo = (x - hi.astype(f32)).astype(bf16)
    return hi, lo


def _split3(x):
    hi = x.astype(bf16)
    r = x - hi.astype(f32)
    mid = r.astype(bf16)
    lo = (r - mid.astype(f32)).astype(bf16)
    return hi, mid, lo


def _dot3(x, m):
    hi, mid, lo = _split3(x)
    return _dot(hi, m) + _dot(mid, m) + _dot(lo, m)


def _dot3_left(m, x):
    hi, mid, lo = _split3(x)
    return _dot(m, hi) + _dot(m, mid) + _dot(m, lo)


def _lane_sum_rows(a):
    ones = jnp.ones((SUBLANES, a.shape[1]), bf16)
    hi, mid, lo = _split3(a)
    return _dot_nt(ones, hi) + _dot_nt(ones, mid) + _dot_nt(ones, lo)


def _softplus(z):
    return jnp.maximum(z, 0.0) + jnp.log1p(jnp.exp(-jnp.abs(z)))


def _log_sigmoid(z):
    return jnp.minimum(z, 0.0) - jnp.log1p(jnp.exp(-jnp.abs(z)))


def _params(n_axes):
    return pltpu.CompilerParams(dimension_semantics=("arbitrary",) * n_axes,
                                vmem_limit_bytes=VMEM_LIMIT)


def _resident(shape):
    nd = len(shape)
    return pl.BlockSpec(shape, lambda *_: (0,) * nd, pipeline_mode=pl.Buffered(1))


def _rope_rows(ref_head, rows, cos, sin):
    x1 = rows[0:ROPE_HALF]
    x2 = rows[ROPE_HALF:ROPE_DIM]
    ref_head[0:ROPE_HALF] = x1 * cos - x2 * sin
    ref_head[ROPE_HALF:ROPE_DIM] = x2 * cos + x1 * sin
    ref_head[ROPE_DIM:] = rows[ROPE_DIM:]


def _inproj_kernel(x_ref, wt_ref, wk_ref, wf_ref, wfr_ref, wg_ref, bf_ref, bfr_ref, cq_ref, sq_ref, cos_ref, sin_ref,
                   qt_ref, krow_ref, kvsb_ref, kvmb_ref, kvfx_ref, logf_ref, cbc_ref, gate_ref, carry_ref, *, tm):
    t = pl.program_id(1)
    x = x_ref[0]
    xb, xlo = _split2(x)
    cos = cos_ref[...]
    sin = sin_ref[...]

    pt = _dot_nt(wt_ref[...], xb)
    head = lambda r: pt[r:r + HEAD_DIM]
    for h in range(H_SB):
        qt_ref[0, h] = head(h * HEAD_DIM)
    for h in range(H_MB):
        _rope_rows(qt_ref.at[0, H_SB + h], head(W_SB + h * HEAD_DIM), cos, sin)
    for h in range(H_FX):
        qt_ref[0, H_SB + H_MB + h] = head(W_SB + W_MB + h * HEAD_DIM)
    base = W_ALL
    for h in range(2 * H_SB):
        kvsb_ref[0, h // H_SB, h % H_SB] = head(base + h * HEAD_DIM)
    base += 2 * W_SB
    for h in range(H_MB):
        _rope_rows(kvmb_ref.at[0, 0, h], head(base + h * HEAD_DIM), cos, sin)
        kvmb_ref[0, 1, h] = head(base + W_MB + h * HEAD_DIM)
    base += 2 * W_MB
    for h in range(2 * H_FX):
        kvfx_ref[0, h // H_FX, h % H_FX] = head(base + h * HEAD_DIM)

    kr = _dot(xb, wk_ref[...])
    k_mb = kr[:, W_SB:W_SB + W_MB] * cq_ref[...] + kr[:, W_ALL:] * sq_ref[...]
    for h in range(H_SB):
        krow_ref[0, h] = kr[:, h * HEAD_DIM:(h + 1) * HEAD_DIM].astype(bf16)
    for h in range(H_MB):
        krow_ref[0, H_SB + h] = k_mb[:, h * HEAD_DIM:(h + 1) * HEAD_DIM].astype(bf16)
    for h in range(H_FX):
        o = W_SB + W_MB + h * HEAD_DIM
        krow_ref[0, H_SB + H_MB + h] = kr[:, o:o + HEAD_DIM].astype(bf16)

    r1 = _dot_nt(wf_ref[...], xb)
    r2 = _dot_nt(wf_ref[0:SUBLANES], xlo)
    logf_ref[0] = _log_sigmoid(r1[0:SUBLANES] + r1[SUBLANES:] + r2 + bf_ref[...])

    f1 = _dot(xb, wfr_ref[...])
    f2 = _dot(xlo, wfr_ref[:, 0:LANES])
    logf_rows = _log_sigmoid(f1[:, 0:LANES] + f1[:, LANES:] + f2 + bfr_ref[...])

    @pl.when(t == 0)
    def _():
        carry_ref[...] = jnp.zeros_like(carry_ref)
    row = lax.broadcasted_iota(jnp.int32, (tm, tm), 0)
    col = lax.broadcasted_iota(jnp.int32, (tm, tm), 1)
    c_rows = _dot3_left((col <= row).astype(bf16), logf_rows) + carry_ref[...]
    carry_ref[...] = c_rows[tm - 1:tm]
    src = lax.broadcasted_iota(jnp.int32, (LANES, LANES), 0)
    for h in range(H_FX):
        cbc_ref[0, h] = _dot3(c_rows, (src == h).astype(bf16))

    g = _dot(xb, wg_ref[...])
    gate_ref[0] = (1.0 / (1.0 + jnp.exp(-g))).astype(gate_ref.dtype)


def _inproj(x, w, rope, tm):
    B, T, D = x.shape
    wt, wk, wf, wfr, wg, bfc, bfr = w
    cq, sq, cos_t, sin_t = rope
    tok = lambda b, t: (b, t, 0)
    out_shape = (
        jax.ShapeDtypeStruct((B, H_ALL, HEAD_DIM, T), f32),
        jax.ShapeDtypeStruct((B, H_ALL, T, HEAD_DIM), bf16),
        jax.ShapeDtypeStruct((B, 2, H_SB, HEAD_DIM, T), f32),
        jax.ShapeDtypeStruct((B, 2, H_MB, HEAD_DIM, T), f32),
        jax.ShapeDtypeStruct((B, 2, H_FX, HEAD_DIM, T), f32),
        jax.ShapeDtypeStruct((B, SUBLANES, T), f32),
        jax.ShapeDtypeStruct((B, H_FX, T, LANES), f32),
        jax.ShapeDtypeStruct((B, T, N_BRANCH * D), bf16),
    )
    kv_spec = lambda H: pl.BlockSpec((1, 2, H, HEAD_DIM, tm), lambda b, t: (b, 0, 0, 0, t))
    return pl.pallas_call(
        functools.partial(_inproj_kernel, tm=tm),
        grid=(B, T // tm),
        in_specs=[pl.BlockSpec((1, tm, D), tok)] + [_resident(a.shape) for a in w]
        + [pl.BlockSpec((tm, W_MB), lambda b, t: (t, 0)),
           pl.BlockSpec((tm, W_MB), lambda b, t: (t, 0)),
           pl.BlockSpec((ROPE_HALF, tm), lambda b, t: (0, t)),
           pl.BlockSpec((ROPE_HALF, tm), lambda b, t: (0, t))],
        out_specs=(pl.BlockSpec((1, H_ALL, HEAD_DIM, tm), lambda b, t: (b, 0, 0, t)),
                   pl.BlockSpec((1, H_ALL, tm, HEAD_DIM), lambda b, t: (b, 0, t, 0)),
                   kv_spec(H_SB), kv_spec(H_MB), kv_spec(H_FX),
                   pl.BlockSpec((1, SUBLANES, tm), lambda b, t: (b, 0, t)),
                   pl.BlockSpec((1, H_FX, tm, LANES), lambda b, t: (b, 0, t, 0)),
                   pl.BlockSpec((1, tm, N_BRANCH * D), tok)),
        out_shape=out_shape,
        scratch_shapes=[pltpu.VMEM((1, LANES), f32)],
        compiler_params=_params(2),
        name="inproj",
    )(x, *w, cq, sq, cos_t, sin_t)


def _tile_iotas(n):
    row = lax.broadcasted_iota(jnp.int32, (n, n), 0)
    col = lax.broadcasted_iota(jnp.int32, (n, n), 1)
    return row, col


def _sb_kernel(qt_ref, k_ref, v_ref, o_ref, acc_ref, carry_ref, *, tq):
    i = pl.program_id(2)
    qb = jnp.transpose(qt_ref[0, 0]).astype(bf16)
    row, col = _tile_iotas(tq)
    later = (row > col).astype(bf16)
    strict = col < row

    def block(j, diag):
        start = pl.multiple_of(j * tq, tq)
        kt = k_ref[0, 0, 0, :, pl.ds(start, tq)].astype(bf16)
        vt = v_ref[0, 0, 0, :, pl.ds(start, tq)].astype(bf16)
        z = _dot(qb, kt)
        l1m = -_softplus(z)
        if diag:
            l1m = jnp.where(strict, l1m, 0.0)
        hi, lo = _split2(l1m)
        after = _dot(hi, later) + _dot(lo, later) + carry_ref[...]
        w = jnp.exp(z + l1m + after)
        if diag:
            w = jnp.where(strict, w, 0.0)
        acc_ref[...] += _dot_nt(w.astype(bf16), vt)
        carry_ref[...] += jnp.sum(l1m, axis=1, keepdims=True)

    acc_ref[...] = jnp.zeros_like(acc_ref)
    carry_ref[...] = jnp.zeros_like(carry_ref)
    block(i, True)

    def cond(s):
        j, top = s
        return jnp.logical_and(j >= 0, top > EXP_ZERO_BELOW)

    def body(s):
        j, _ = s
        block(j, False)
        return j - 1, jnp.max(carry_ref[...])

    lax.while_loop(cond, body, (i - 1, jnp.max(carry_ref[...])))
    o_ref[0, 0] = acc_ref[...].astype(o_ref.dtype)


def _softmax_init(m_ref, l_ref, acc_ref):
    m_ref[...] = jnp.full_like(m_ref, NEG)
    l_ref[...] = jnp.zeros_like(l_ref)
    acc_ref[...] = jnp.zeros_like(acc_ref)


def _softmax_step(st, vt, m_ref, l_ref, acc_ref):
    m_old = m_ref[...]
    m_new = jnp.maximum(m_old, jnp.max(st, axis=0, keepdims=True))
    alpha = jnp.exp(m_old - m_new)
    p = jnp.exp(st - m_new)
    l_ref[...] = alpha * l_ref[...] + jnp.sum(p, axis=0, keepdims=True)
    acc_ref[...] = alpha * acc_ref[...] + _dot(vt, p.astype(bf16))
    m_ref[...] = m_new


def _softmax_finish(o_ref, l_ref, acc_ref):
    o_ref[0, 0] = jnp.transpose(acc_ref[...] / l_ref[...]).astype(o_ref.dtype)


def _causal(chunk_start, query_start, tk, tq):
    row = lax.broadcasted_iota(jnp.int32, (tk, tq), 0)
    col = lax.broadcasted_iota(jnp.int32, (tk, tq), 1)
    return row - col <= query_start - chunk_start


def _fox_kernel(qt_ref, kr_ref, v_ref, cbc_ref, o_ref, kn_ref, m_ref, l_ref, acc_ref, *, tq, kb):
    i = pl.program_id(2)
    tk = kb * tq
    n_keys = kr_ref.shape[2]

    @pl.when(i == 0)
    def _():
        def body(n, top):
            k = kr_ref[0, 0, pl.ds(pl.multiple_of(n * NORM_ROWS, NORM_ROWS), NORM_ROWS), :].astype(f32)
            return jnp.maximum(top, jnp.max(jnp.sum(k * k, axis=1, keepdims=True), axis=0, keepdims=True))
        kn_ref[...] = jnp.sqrt(lax.fori_loop(0, n_keys // NORM_ROWS, body, jnp.zeros((1, 1), f32)))

    qt = qt_ref[0, 0]
    qtb = qt.astype(bf16)
    q_norm = jnp.sqrt(jnp.max(jnp.sum(qt * qt, axis=0, keepdims=True), axis=1, keepdims=True))
    qk_bound = 1.01 * q_norm * kn_ref[...] + 0.01

    c_base = cbc_ref[0, 0, pl.ds(pl.multiple_of(i * tq, tq), 1), :]

    def chunk(c, diag):
        start = pl.multiple_of(c * tk, tk)
        st = _dot(kr_ref[0, 0, pl.ds(start, tk), :], qtb)
        cb = cbc_ref[0, 0, pl.ds(start, tk), :] - c_base
        st = st - jnp.concatenate([cb] * (tq // LANES), axis=1)
        if diag:
            st = jnp.where(_causal(start, i * tq, tk, tq), st, NEG)
        vt = v_ref[0, 0, 0, :, pl.ds(start, tk)].astype(bf16)
        _softmax_step(st, vt, m_ref, l_ref, acc_ref)

    def reaches(c):
        last = jnp.maximum((c + 1) * tk - 1, 0)
        c_last = cbc_ref[0, 0, pl.ds(last, 1), :] - c_base
        m_low = jnp.min(m_ref[...], axis=1, keepdims=True)
        return jnp.max(qk_bound - c_last - m_low) >= EXP_ZERO_BELOW

    _softmax_init(m_ref, l_ref, acc_ref)
    c0 = i // kb
    chunk(c0, True)

    def cond(s):
        c, go = s
        return jnp.logical_and(c >= 0, go)

    def body(s):
        c, _ = s
        chunk(c, False)
        return c - 1, reaches(c - 1)

    lax.while_loop(cond, body, (c0 - 1, reaches(c0 - 1)))
    _softmax_finish(o_ref, l_ref, acc_ref)


def _top_blocks(gate, cand, idx, axis):
    n = gate.shape[axis]
    gate = jnp.where(cand, gate, NEG)
    chosen = jnp.zeros(gate.shape, f32)
    firsts = []
    for _ in range(MOBA_TOPK):
        top = jnp.max(gate, axis=axis, keepdims=True)
        first = jnp.min(jnp.where(gate == top, idx, n), axis=axis, keepdims=True)
        pick = idx == first
        chosen = jnp.where(jnp.logical_and(pick, cand), 1.0, chosen)
        gate = jnp.where(pick, -jnp.inf, gate)
        firsts.append(first)
    return chosen, firsts


def _moba_kernel(qt_ref, kr_ref, k_ref, v_ref, o_ref, km_ref, ch_ref, m_ref, l_ref, acc_ref, *, tq, kb, nb):
    i = pl.program_id(2)
    tk = kb * tq

    @pl.when(i == 0)
    def _():
        lane = lax.broadcasted_iota(jnp.int32, (HEAD_DIM, LANES), 1)

        def body(j, km):
            start = pl.multiple_of(j * tq, tq)
            blk = k_ref[0, 0, 0, :, pl.ds(start, tq)]
            mean = jnp.sum(blk, axis=1, keepdims=True) * (1.0 / tq)
            return jnp.where(lane == j, mean, km)

        km_ref[...] = jnp.transpose(lax.fori_loop(0, nb, body, jnp.zeros((HEAD_DIM, LANES), f32)))

    qt = qt_ref[0, 0]
    qtb = qt.astype(bf16)
    qh, ql = _split2(qt)
    kh, kl = _split2(km_ref[...])
    gate = _dot(kh, qh) + _dot(kh, ql) + _dot(kl, qh)
    blk = lax.broadcasted_iota(jnp.int32, (LANES, tq), 0)
    chosen, _ = _top_blocks(gate, blk < i, blk, 0)
    ch_ref[...] = jnp.where(blk == i, 1.0, chosen)

    def chunk(c, diag):
        start = pl.multiple_of(c * tk, tk)
        st = _dot(kr_ref[0, 0, pl.ds(start, tk), :], qtb)
        picked = jnp.concatenate(
            [jnp.broadcast_to(ch_ref[pl.ds(c * kb + t, 1), :], (tq, tq)) for t in range(kb)], axis=0) > 0.0
        if diag:
            picked = jnp.logical_and(picked, _causal(start, i * tq, tk, tq))
        vt = v_ref[0, 0, 0, :, pl.ds(start, tk)].astype(bf16)
        _softmax_step(jnp.where(picked, st, NEG), vt, m_ref, l_ref, acc_ref)

    _softmax_init(m_ref, l_ref, acc_ref)
    c0 = i // kb
    chunk(c0, True)

    def body(c, _):
        chunk(c, False)
        return 0

    lax.fori_loop(0, c0, body, 0)
    _softmax_finish(o_ref, l_ref, acc_ref)


def _softmax_scratch(tq):
    return [pltpu.VMEM((1, tq), f32), pltpu.VMEM((1, tq), f32), pltpu.VMEM((HEAD_DIM, tq), f32)]


def _prompt_attention(kernel, name, head_off, n_heads, operands, specs, scratch):
    qt = operands[0]
    B, _, hd, T = qt.shape
    return pl.pallas_call(
        kernel,
        grid=(B, n_heads, T // TQ),
        in_specs=[pl.BlockSpec((1, 1, hd, TQ), lambda b, h, i: (b, h + head_off, 0, i))] + specs,
        out_specs=pl.BlockSpec((1, 1, TQ, hd), lambda b, h, i: (b, h, i, 0)),
        out_shape=jax.ShapeDtypeStruct((B, n_heads, T, hd), bf16),
        scratch_shapes=scratch,
        compiler_params=_params(3),
        name=name,
    )(*operands)


def _kvt_spec(T, kv):
    return pl.BlockSpec((1, 1, 1, HEAD_DIM, T), lambda b, h, i: (b, kv, h, 0, 0))


def _krow_spec(T, head_off):
    return pl.BlockSpec((1, 1, T, HEAD_DIM), lambda b, h, i: (b, h + head_off, 0, 0))


def _sb_prompt(qt, kvt):
    T = qt.shape[3]
    return _prompt_attention(functools.partial(_sb_kernel, tq=TQ), "sb_prompt", 0, H_SB,
                             (qt, kvt, kvt), [_kvt_spec(T, 0), _kvt_spec(T, 1)],
                             [pltpu.VMEM((TQ, HEAD_DIM), f32), pltpu.VMEM((TQ, 1), f32)])


def _moba_prompt(qt, krow, kvt):
    T = qt.shape[3]
    assert T % (MOBA_CHUNK_BLOCKS * TQ) == 0 and T // TQ <= LANES
    return _prompt_attention(functools.partial(_moba_kernel, tq=TQ, kb=MOBA_CHUNK_BLOCKS, nb=T // TQ),
                             "moba_prompt", H_SB, H_MB, (qt, krow, kvt, kvt),
                             [_krow_spec(T, H_SB), _kvt_spec(T, 0), _kvt_spec(T, 1)],
                             [pltpu.VMEM((LANES, HEAD_DIM), f32), pltpu.VMEM((LANES, TQ), f32)] + _softmax_scratch(TQ))


def _fox_prompt(qt, krow, kvt, cbc):
    T = qt.shape[3]
    assert T % (FOX_CHUNK_BLOCKS * TQ) == 0 and T % NORM_ROWS == 0
    return _prompt_attention(functools.partial(_fox_kernel, tq=TQ, kb=FOX_CHUNK_BLOCKS), "fox_prompt",
                             H_SB + H_MB, H_FX, (qt, krow, kvt, cbc),
                             [_krow_spec(T, H_SB + H_MB), _kvt_spec(T, 1),
                              pl.BlockSpec((1, 1, T, LANES), lambda b, h, i: (b, h, 0, 0))],
                             [pltpu.VMEM((1, 1), f32)] + _softmax_scratch(TQ))


def _layer_norm(x, g, b):
    mu = jnp.mean(x, axis=-1, keepdims=True)
    xc = x - mu
    var = jnp.mean(xc * xc, axis=-1, keepdims=True)
    return xc * lax.rsqrt(var + LN_EPS) * g + b


def _post_kernel(x_ref, osb_ref, omb_ref, ofx_ref, gate_ref, wsb_ref, wmb_ref, wfx_ref, wo_ref, wup_ref, wdn_ref,
                 g1_ref, b1_ref, g2_ref, b2_ref, y_ref):
    x = x_ref[0]
    merged = None
    for br, (o_ref, w_ref) in enumerate(((osb_ref, wsb_ref), (omb_ref, wmb_ref), (ofx_ref, wfx_ref))):
        pb = None
        for h in range(o_ref.shape[1]):
            t = _dot(o_ref[0, h], w_ref[h])
            pb = t if pb is None else pb + t
        gb = gate_ref[0, :, br * D_MODEL:(br + 1) * D_MODEL].astype(f32)
        merged = gb * pb if merged is None else merged + gb * pb
    u = _dot(merged.astype(bf16), wo_ref[...])
    x1 = _layer_norm(DN_ALPHA * x + u, g1_ref[...], b1_ref[...])
    x1b = x1.astype(bf16)
    acc = None
    for f in range(D_FF // FF_CHUNK):
        hm = jnp.maximum(_dot(x1b, wup_ref[:, f * FF_CHUNK:(f + 1) * FF_CHUNK]), 0.0)
        t = _dot((hm * hm).astype(bf16), wdn_ref[f * FF_CHUNK:(f + 1) * FF_CHUNK, :])
        acc = t if acc is None else acc + t
    y_ref[0] = _layer_norm(DN_ALPHA * x1 + acc, g2_ref[...], b2_ref[...])


def _post(x, o_sb, o_mb, o_fx, gates, w, tm):
    B, T, D = x.shape
    tok = lambda b, t: (b, t, 0)
    o_spec = lambda H: pl.BlockSpec((1, H, tm, HEAD_DIM), lambda b, t: (b, 0, t, 0))
    return pl.pallas_call(
        _post_kernel,
        grid=(B, T // tm),
        in_specs=[pl.BlockSpec((1, tm, D), tok), o_spec(H_SB), o_spec(H_MB), o_spec(H_FX),
                  pl.BlockSpec((1, tm, N_BRANCH * D), tok)] + [_resident(a.shape) for a in w],
        out_specs=pl.BlockSpec((1, tm, D), tok),
        out_shape=jax.ShapeDtypeStruct((B, T, D), f32),
        compiler_params=_params(2),
        name="post",
    )(x, o_sb, o_mb, o_fx, gates, *w)


def _page_scores(z_ref, k_page, qb_ref, n_heads):
    for h in range(n_heads):
        z_ref[h:h + 1, :] = jnp.sum(k_page[h] * qb_ref[0, h], axis=0, keepdims=True)


def _later_matrix():
    row, col = _tile_iotas(PAGE)
    return (row > col).astype(bf16)


def _finish_heads(o_ref, acc_ref, n_heads, scale=None):
    o_ref[0] = jnp.zeros(o_ref.shape[1:], f32)
    for h in range(n_heads):
        r = _lane_sum_rows(acc_ref[h])[0:1]
        if scale is not None:
            r = r * scale[h:h + 1, 0:HEAD_DIM]
        o_ref[0, h:h + 1, :] = r


def _sb_dec_kernel(pt_ref, qb_ref, *refs, n_heads):
    pages = refs[:PAGES_PER_STEP]
    o_ref, z_ref, acc_ref, carry_ref = refs[PAGES_PER_STEP:]
    c = pl.program_id(1)

    @pl.when(c == 0)
    def _():
        z_ref[...] = jnp.zeros_like(z_ref)
        acc_ref[...] = jnp.zeros_like(acc_ref)
        carry_ref[...] = jnp.zeros_like(carry_ref)

    later = _later_matrix()
    for page in pages:
        @pl.when(jnp.max(carry_ref[0:n_heads]) > EXP_ZERO_BELOW)
        def _():
            _page_scores(z_ref, page.at[0], qb_ref, n_heads)
            z = z_ref[...]
            l1m = -_softplus(z)
            hi, lo = _split2(l1m)
            after = _dot(hi, later) + _dot(lo, later) + carry_ref[...]
            w = jnp.exp(z + l1m + after)
            for h in range(n_heads):
                acc_ref[h] += page[1, h] * w[h:h + 1, :]
            carry_ref[...] += jnp.sum(l1m, axis=1, keepdims=True)

    @pl.when(c == pl.num_programs(1) - 1)
    def _():
        _finish_heads(o_ref, acc_ref, n_heads)


def _fox_dec_kernel(pt_ref, qb_ref, kb_ref, vb_ref, lfn_ref, *refs, n_heads):
    pages = refs[:PAGES_PER_STEP]
    lf_pages = refs[PAGES_PER_STEP:2 * PAGES_PER_STEP]
    o_ref, z_ref, lf_ref, m_ref, l_ref, acc_ref, carry_ref = refs[2 * PAGES_PER_STEP:]
    c = pl.program_id(1)

    @pl.when(c == 0)
    def _():
        z_ref[...] = jnp.zeros_like(z_ref)
        lf_ref[...] = jnp.zeros_like(lf_ref)
        carry_ref[...] = jnp.zeros_like(carry_ref)
        _page_scores(z_ref, kb_ref.at[0], qb_ref, n_heads)
        m_ref[...] = z_ref[...]
        l_ref[...] = jnp.ones_like(l_ref)
        lane0 = lax.broadcasted_iota(jnp.int32, (HEAD_DIM, LANES), 1) == 0
        for h in range(n_heads):
            acc_ref[h] = jnp.where(lane0, vb_ref[0, h], 0.0)

    later = _later_matrix()
    for page, lf_page in zip(pages, lf_pages):
        _page_scores(z_ref, page.at[0], qb_ref, n_heads)
        for h in range(n_heads):
            lf_ref[h:h + 1, :] = lf_page[h]
        lf = lf_ref[...]
        s = z_ref[...] + _dot3(lf, later) + carry_ref[...] + lfn_ref[0]
        m_old = m_ref[...]
        m_new = jnp.maximum(m_old, jnp.max(s, axis=1, keepdims=True))
        alpha = jnp.exp(m_old - m_new)
        p = jnp.exp(s - m_new)
        l_ref[...] = alpha * l_ref[...] + jnp.sum(p, axis=1, keepdims=True)
        m_ref[...] = m_new
        for h in range(n_heads):
            acc_ref[h] = acc_ref[h] * alpha[h:h + 1, :] + page[1, h] * p[h:h + 1, :]
        carry_ref[...] += jnp.sum(lf, axis=1, keepdims=True)

    @pl.when(c == pl.num_programs(1) - 1)
    def _():
        _finish_heads(o_ref, acc_ref, n_heads, scale=1.0 / l_ref[...])


def _cache_view(cache):
    return jnp.transpose(cache, (0, 1, 3, 4, 5, 2))


def _bcast_spec(n_heads):
    return pl.BlockSpec((1, n_heads, HEAD_DIM, LANES), lambda b, c, pt: (b, 0, 0, 0))


def _page_specs(layer, n_heads, n_pages, newest_first, kv=None):
    specs = []
    for i in range(PAGES_PER_STEP):
        def page_of(b, c, pt, i=i):
            lp = c * PAGES_PER_STEP + i
            return pt[b, n_pages - 1 - lp] if newest_first else pt[b, lp]
        if kv is None:
            specs.append(pl.BlockSpec((None, None, 2, n_heads, HEAD_DIM, PAGE),
                                      lambda b, c, pt, f=page_of: (layer, f(b, c, pt), 0, 0, 0, 0)))
        else:
            specs.append(pl.BlockSpec((None, None, None, n_heads, HEAD_DIM, PAGE),
                                      lambda b, c, pt, f=page_of: (layer, f(b, c, pt), kv, 0, 0, 0)))
    return specs


def _dec_out(nseq):
    return (pl.BlockSpec((1, SUBLANES, HEAD_DIM), lambda b, c, pt: (b, 0, 0)),
            jax.ShapeDtypeStruct((nseq, SUBLANES, HEAD_DIM), f32))


def _sb_decode(cache_t, layer, page_table, qb):
    nseq, n_pages = page_table.shape
    n_heads = cache_t.shape[3]
    out_spec, out_shape = _dec_out(nseq)
    gs = pltpu.PrefetchScalarGridSpec(
        num_scalar_prefetch=1, grid=(nseq, n_pages // PAGES_PER_STEP),
        in_specs=[_bcast_spec(n_heads)] + _page_specs(layer, n_heads, n_pages, True),
        out_specs=out_spec,
        scratch_shapes=[pltpu.VMEM((SUBLANES, PAGE), f32), pltpu.VMEM((n_heads, HEAD_DIM, PAGE), f32),
                        pltpu.VMEM((SUBLANES, 1), f32)])
    return pl.pallas_call(functools.partial(_sb_dec_kernel, n_heads=n_heads), grid_spec=gs, out_shape=out_shape,
                          compiler_params=_params(2), name="sb_decode",
                          )(page_table, qb, *([cache_t] * PAGES_PER_STEP))


def _fox_decode(cache_t, logf_t, layer, page_table, qb, kb, vb, lfn):
    nseq, n_pages = page_table.shape
    n_heads = cache_t.shape[3]
    out_spec, out_shape = _dec_out(nseq)
    lf_specs = []
    for i in range(PAGES_PER_STEP):
        lf_specs.append(pl.BlockSpec(
            (None, n_heads, None, 1, PAGE),
            lambda b, c, pt, i=i: (layer, 0, pt[b, n_pages - 1 - (c * PAGES_PER_STEP + i)], 0, 0)))
    gs = pltpu.PrefetchScalarGridSpec(
        num_scalar_prefetch=1, grid=(nseq, n_pages // PAGES_PER_STEP),
        in_specs=[_bcast_spec(n_heads)] * 3 + [pl.BlockSpec((1, SUBLANES, LANES), lambda b, c, pt: (b, 0, 0))]
        + _page_specs(layer, n_heads, n_pages, True) + lf_specs,
        out_specs=out_spec,
        scratch_shapes=[pltpu.VMEM((SUBLANES, PAGE), f32), pltpu.VMEM((SUBLANES, PAGE), f32),
                        pltpu.VMEM((SUBLANES, PAGE), f32), pltpu.VMEM((SUBLANES, PAGE), f32),
                        pltpu.VMEM((n_heads, HEAD_DIM, PAGE), f32), pltpu.VMEM((SUBLANES, 1), f32)])
    return pl.pallas_call(functools.partial(_fox_dec_kernel, n_heads=n_heads), grid_spec=gs, out_shape=out_shape,
                          compiler_params=_params(2), name="fox_decode",
                          )(page_table, qb, kb, vb, lfn, *([cache_t] * PAGES_PER_STEP),
                            *([logf_t] * PAGES_PER_STEP))


def _mb_scores_kernel(pt_ref, qb_ref, kb_ref, *refs, n_heads):
    pages = refs[:PAGES_PER_STEP]
    z_out, bsum_ref, zself_ref, z_ref = refs[PAGES_PER_STEP:]
    c = pl.program_id(1)

    @pl.when(c == 0)
    def _():
        z_ref[...] = jnp.zeros_like(z_ref)
        bsum_ref[...] = jnp.zeros_like(bsum_ref)
        _page_scores(z_ref, kb_ref.at[0], qb_ref, n_heads)
        zself_ref[0] = z_ref[...]

    lane = lax.broadcasted_iota(jnp.int32, (SUBLANES, LANES), 1)
    for i, page in enumerate(pages):
        _page_scores(z_ref, page, qb_ref, n_heads)
        z = z_ref[...]
        z_out[0, :, i * PAGE:(i + 1) * PAGE] = z
        blk = (c * PAGES_PER_STEP + i) // PAGES_PER_BLOCK
        bsum_ref[0] += jnp.where(lane == blk, jnp.sum(z, axis=1, keepdims=True), 0.0)


def _mb_select_kernel(z_ref, bsum_ref, zself_ref, p_ref, pself_ref, idx_ref, *, n_blocks):
    gate = bsum_ref[0] * (1.0 / MOBA_BLOCK)
    lane = lax.broadcasted_iota(jnp.int32, (SUBLANES, LANES), 1)
    _, firsts = _top_blocks(gate, lane < n_blocks, lane, 1)
    idx = jnp.zeros((SUBLANES, LANES), jnp.int32)
    for r, first in enumerate(firsts):
        idx = jnp.where(lane == r, first, idx)
    idx_ref[0] = idx
    n_keys = z_ref.shape[2]
    blk_of_key = lax.broadcasted_iota(jnp.int32, (SUBLANES, n_keys), 1) // MOBA_BLOCK
    picked = None
    for first in firsts:
        hit = jnp.logical_and(blk_of_key == first, first < n_blocks)
        picked = hit if picked is None else jnp.logical_or(picked, hit)
    s = jnp.where(picked, z_ref[0], NEG)
    zs = zself_ref[0]
    m = jnp.maximum(jnp.max(s, axis=1, keepdims=True), zs[:, 0:1])
    p = jnp.exp(s - m)
    ps = jnp.exp(zs - m)
    inv = 1.0 / (jnp.sum(p, axis=1, keepdims=True) + ps[:, 0:1])
    p_ref[0] = p * inv
    pself_ref[0] = ps * inv


N_PICKED_PAGES = MOBA_TOPK * PAGES_PER_BLOCK


def _mb_pv_kernel(pt_ref, idx_ref, *refs):
    v_pages = refs[:N_PICKED_PAGES]
    p_pages = refs[N_PICKED_PAGES:2 * N_PICKED_PAGES]
    pself_ref, vb_ref, o_ref = refs[2 * N_PICKED_PAGES:]
    h = pl.program_id(1)

    @pl.when(h == 0)
    def _():
        o_ref[...] = jnp.zeros_like(o_ref)

    lane0 = lax.broadcasted_iota(jnp.int32, (HEAD_DIM, LANES), 1) == 0
    acc = jnp.where(lane0, vb_ref[0, h] * pself_ref[0, pl.ds(h, 1), :], 0.0)
    for v_page, p_page in zip(v_pages, p_pages):
        acc = acc + v_page[...] * p_page[0, pl.ds(h, 1), :]
    o_ref[0, pl.ds(h, 1), :] = _lane_sum_rows(acc)[0:1]


def _moba_decode(cache_t, layer, page_table, qb, kb, vb):
    nseq, n_pages = page_table.shape
    n_heads = cache_t.shape[3]
    n_keys = n_pages * PAGE
    n_blocks = n_keys // MOBA_BLOCK
    assert MOBA_TOPK <= n_blocks <= LANES
    row8 = pl.BlockSpec((1, SUBLANES, LANES), lambda b, c, pt: (b, 0, 0))
    gs = pltpu.PrefetchScalarGridSpec(
        num_scalar_prefetch=1, grid=(nseq, n_pages // PAGES_PER_STEP),
        in_specs=[_bcast_spec(n_heads)] * 2 + _page_specs(layer, n_heads, n_pages, False, kv=0),
        out_specs=(pl.BlockSpec((1, SUBLANES, PAGES_PER_STEP * PAGE), lambda b, c, pt: (b, 0, c)), row8, row8),
        scratch_shapes=[pltpu.VMEM((SUBLANES, PAGE), f32)])
    z, bsum, zself = pl.pallas_call(
        functools.partial(_mb_scores_kernel, n_heads=n_heads), grid_spec=gs,
        out_shape=(jax.ShapeDtypeStruct((nseq, SUBLANES, n_keys), f32),
                   jax.ShapeDtypeStruct((nseq, SUBLANES, LANES), f32),
                   jax.ShapeDtypeStruct((nseq, SUBLANES, LANES), f32)),
        compiler_params=_params(2), name="moba_decode_scores",
    )(page_table, qb, kb, *([cache_t] * PAGES_PER_STEP))

    keys = pl.BlockSpec((1, SUBLANES, n_keys), lambda b: (b, 0, 0))
    small = pl.BlockSpec((1, SUBLANES, LANES), lambda b: (b, 0, 0))
    p, pself, idx = pl.pallas_call(
        functools.partial(_mb_select_kernel, n_blocks=n_blocks),
        grid=(nseq,), in_specs=[keys, small, small], out_specs=(keys, small, small),
        out_shape=(jax.ShapeDtypeStruct((nseq, SUBLANES, n_keys), f32),
                   jax.ShapeDtypeStruct((nseq, SUBLANES, LANES), f32),
                   jax.ShapeDtypeStruct((nseq, SUBLANES, LANES), jnp.int32)),
        compiler_params=_params(1), name="moba_decode_select",
    )(z, bsum, zself)

    idx_flat = idx[:, :, :MOBA_TOPK].reshape(nseq, SUBLANES * MOBA_TOPK)

    def page_no(b, h, k, idx):
        return idx[b, h * MOBA_TOPK + k // PAGES_PER_BLOCK] * PAGES_PER_BLOCK + k % PAGES_PER_BLOCK

    v_specs = [pl.BlockSpec((None, None, None, None, HEAD_DIM, PAGE),
                            lambda b, h, pt, idx, k=k: (layer, pt[b, page_no(b, h, k, idx)], 1, h, 0, 0))
               for k in range(N_PICKED_PAGES)]
    p_specs = [pl.BlockSpec((1, SUBLANES, PAGE), lambda b, h, pt, idx, k=k: (b, 0, page_no(b, h, k, idx)))
               for k in range(N_PICKED_PAGES)]
    gs = pltpu.PrefetchScalarGridSpec(
        num_scalar_prefetch=2, grid=(nseq, n_heads),
        in_specs=v_specs + p_specs
        + [pl.BlockSpec((1, SUBLANES, LANES), lambda b, h, pt, idx: (b, 0, 0)),
           pl.BlockSpec((1, n_heads, HEAD_DIM, LANES), lambda b, h, pt, idx: (b, 0, 0, 0))],
        out_specs=pl.BlockSpec((1, SUBLANES, HEAD_DIM), lambda b, h, pt, idx: (b, 0, 0)))
    return pl.pallas_call(
        _mb_pv_kernel, grid_spec=gs, out_shape=jax.ShapeDtypeStruct((nseq, SUBLANES, HEAD_DIM), f32),
        compiler_params=_params(2), name="moba_decode_pv",
    )(page_table, idx_flat, *([cache_t] * N_PICKED_PAGES), *([p] * N_PICKED_PAGES), pself, vb)


def _rope_tables(pos):
    inv_freq = ROPE_THETA ** (-jnp.arange(ROPE_HALF, dtype=f32) / ROPE_HALF)
    ang = pos.astype(f32)[:, None] * inv_freq[None, :]
    cos, sin = jnp.cos(ang), jnp.sin(ang)
    n = pos.shape[0]
    rest = HEAD_DIM - ROPE_DIM
    cq = jnp.tile(jnp.concatenate([cos, cos, jnp.ones((n, rest), f32)], axis=1), (1, H_MB))
    sq = jnp.tile(jnp.concatenate([sin, sin, jnp.zeros((n, rest), f32)], axis=1), (1, H_MB))
    return cq, sq, cos.T, sin.T


def _hi_lo(w):
    hi = w.astype(bf16)
    return hi, (w - hi.astype(f32)).astype(bf16)


def _layer_weights(w_in, b_forget):
    scale = HEAD_DIM ** -0.5
    edges = [0]
    for wdt in (W_SB,) * 3 + (W_MB,) * 3 + (W_FX,) * 3 + (H_FX,):
        edges.append(edges[-1] + wdt)
    seg = [w_in[:, a:b] for a, b in zip(edges[:-1], edges[1:])]
    q_sb, k_sb, v_sb, q_mb, k_mb, v_mb, q_fx, k_fx, v_fx, w_f = seg
    w_g = w_in[:, edges[-1]:]
    km = k_mb.reshape(D_MODEL, H_MB, HEAD_DIM)
    k_sw = jnp.concatenate([-km[..., ROPE_HALF:ROPE_DIM], km[..., :ROPE_HALF],
                            jnp.zeros((D_MODEL, H_MB, HEAD_DIM - ROPE_DIM), f32)], axis=-1).reshape(D_MODEL, W_MB)
    wt = jnp.concatenate([q_sb * scale, q_mb * scale, q_fx * scale,
                          k_sb, v_sb, k_mb, v_mb, k_fx, v_fx], axis=1).T.astype(bf16)
    wk = jnp.concatenate([k_sb, k_mb, k_fx, k_sw], axis=1).astype(bf16)
    wf_hi, wf_lo = _hi_lo(jnp.pad(w_f, ((0, 0), (0, SUBLANES - H_FX))).T)
    wf = jnp.concatenate([wf_hi, wf_lo], axis=0)
    wfr_hi, wfr_lo = _hi_lo(jnp.pad(w_f, ((0, 0), (0, LANES - H_FX))))
    wfr = jnp.concatenate([wfr_hi, wfr_lo], axis=1)
    bfc = jnp.pad(b_forget, (0, SUBLANES - H_FX)).reshape(SUBLANES, 1)
    bfr = jnp.pad(b_forget, (0, LANES - H_FX)).reshape(1, LANES)
    return wt, wk, wf, wfr, w_g.astype(bf16), bfc, bfr


def _post_weights(w_br_sb, w_br_mb, w_br_fx, w_o, ln1_g, ln1_b, w_up, w_down, ln2_g, ln2_b):
    heads = lambda w: w.reshape(-1, HEAD_DIM, D_MODEL).astype(bf16)
    vec = lambda v: v.reshape(1, D_MODEL)
    return (heads(w_br_sb), heads(w_br_mb), heads(w_br_fx), w_o.astype(bf16), w_up.astype(bf16),
            w_down.astype(bf16), vec(ln1_g), vec(ln1_b), vec(ln2_g), vec(ln2_b))


def _lane_bcast(a):
    return jnp.broadcast_to(a[..., None], a.shape + (LANES,))


def kernel(x_prompt, x_sample, cache_kv_sb, cache_kv_moba, cache_kv_fox, cache_logf_fox, page_table,
           w_in, b_forget, w_br_sb, w_br_moba, w_br_fox, w_o, ln1_g, ln1_b, w_up, w_down, ln2_g, ln2_b):
    B, T, D = x_prompt.shape
    nseq = x_sample.shape[0]
    n_pages = page_table.shape[1]
    past_len = n_pages * PAGE

    rope_p = _rope_tables(jnp.arange(T, dtype=jnp.int32))
    rope_s = _rope_tables(jnp.full((nseq,), past_len, jnp.int32))
    sb_t, mb_t, fx_t = _cache_view(cache_kv_sb), _cache_view(cache_kv_moba), _cache_view(cache_kv_fox)
    pool = cache_logf_fox.shape[1]
    logf_t = jnp.transpose(cache_logf_fox, (0, 3, 1, 2)).reshape(DEPTH, H_FX, pool, 1, PAGE)

    yp = x_prompt
    ys = x_sample.reshape(1, nseq, D)
    rows_p, rows_s = [], []
    for l in range(DEPTH):
        w1 = _layer_weights(w_in[l], b_forget[l])
        w2 = _post_weights(w_br_sb[l], w_br_moba[l], w_br_fox[l], w_o[l], ln1_g[l], ln1_b[l],
                           w_up[l], w_down[l], ln2_g[l], ln2_b[l])

        qt, krow, kvsb, kvmb, kvfx, logf, cbc, gates = _inproj(yp, w1, rope_p, TM_INPROJ)
        o_sb = _sb_prompt(qt, kvsb)
        o_mb = _moba_prompt(qt, krow, kvmb)
        o_fx = _fox_prompt(qt, krow, kvfx, cbc)
        yp = _post(yp, o_sb, o_mb, o_fx, gates, w2, TM_POST)
        rows_p.append((kvsb, kvmb, kvfx, logf))

        qt, _, kvsb, kvmb, kvfx, logf, _, gates = _inproj(ys, w1, rope_s, nseq)
        qd = _lane_bcast(jnp.transpose(qt[0], (2, 0, 1)))
        new = lambda kvt, j: _lane_bcast(jnp.transpose(kvt[0, j], (2, 0, 1)))
        lfn = _lane_bcast(logf[0].T)
        o_sb = _sb_decode(sb_t, l, page_table, qd[:, :H_SB])
        o_mb = _moba_decode(mb_t, l, page_table, qd[:, H_SB:H_SB + H_MB], new(kvmb, 0), new(kvmb, 1))
        o_fx = _fox_decode(fx_t, logf_t, l, page_table, qd[:, H_SB + H_MB:], new(kvfx, 0), new(kvfx, 1), lfn)
        heads = lambda o, H: jnp.transpose(o[:, :H], (1, 0, 2))[None].astype(bf16)
        ys = _post(ys, heads(o_sb, H_SB), heads(o_mb, H_MB), heads(o_fx, H_FX), gates, w2, nseq)
        rows_s.append((kvsb, kvmb, kvfx, logf))

    def kv_out(rows, j):
        return jnp.transpose(jnp.stack([r[j] for r in rows]), (0, 1, 5, 2, 3, 4))

    def logf_out(rows):
        return jnp.transpose(jnp.stack([r[3] for r in rows])[:, :, :H_FX], (0, 1, 3, 2))

    def sample(a):
        return jnp.swapaxes(a, 1, 2)

    return (yp, ys.reshape(nseq, 1, D),
            kv_out(rows_p, 0), kv_out(rows_p, 1), kv_out(rows_p, 2), logf_out(rows_p),
            sample(kv_out(rows_s, 0)), sample(kv_out(rows_s, 1)), sample(kv_out(rows_s, 2)),
            sample(logf_out(rows_s)))
```

```python
import functools

import jax
import jax.numpy as jnp
from jax import lax
from jax.experimental import pallas as pl
from jax.experimental.pallas import tpu as pltpu

bf16 = jnp.bfloat16
f32 = jnp.float32

D_MODEL = 1024
HEAD_DIM = 64
H_SB, H_MB, H_FX = 6, 5, 5
H_ALL = H_SB + H_MB + H_FX
W_SB, W_MB, W_FX = H_SB * HEAD_DIM, H_MB * HEAD_DIM, H_FX * HEAD_DIM
W_ALL = W_SB + W_MB + W_FX
N_BRANCH = 3
MOBA_BLOCK = 256
MOBA_TOPK = 3
ROPE_DIM = HEAD_DIM // 4
ROPE_HALF = ROPE_DIM // 2
ROPE_THETA = 500000.0
D_FF = 4 * D_MODEL
DEPTH = 2
DN_ALPHA = (2 * DEPTH) ** 0.25
LN_EPS = 1e-5
PAGE = 128
SUBLANES = 8
LANES = 128
VMEM_LIMIT = 56 << 20
NEG = -1e30
EXP_ZERO_BELOW = -104.0
TM_INPROJ = 256
TM_POST = 512
FF_CHUNK = 1024
TQ = MOBA_BLOCK
SB_HEAD_GROUP = 2
FOX_CHUNK_BLOCKS = 2
MOBA_CHUNK_BLOCKS = 4
NORM_ROWS = 1024
PAGES_PER_STEP = 16
PAGES_PER_BLOCK = MOBA_BLOCK // PAGE

NT_DIMS = (((1,), (1,)), ((), ()))


def _dot(a, b):
    return jnp.dot(a, b, preferred_element_type=f32)


def _dot_nt(a, b):
    return lax.dot_general(a, b, NT_DIMS, preferred_element_type=f32)


def _split2(x):
    hi = x.astype(bf16)
    lo = (x - hi.astype(f32)).astype(bf16)
    return hi, lo


def _split3(x):
    hi = x.astype(bf16)
    r = x - hi.astype(f32)
    mid = r.astype(bf16)
    lo = (r - mid.astype(f32)).astype(bf16)
    return hi, mid, lo


def _dot3(x, m):
    hi, mid, lo = _split3(x)
    return _dot(hi, m) + _dot(mid, m) + _dot(lo, m)


def _dot3_left(m, x):
    hi, mid, lo = _split3(x)
    return _dot(m, hi) + _dot(m, mid) + _dot(m, lo)


def _lane_sum_rows(a):
    ones = jnp.ones((SUBLANES, a.shape[1]), bf16)
    hi, mid, lo = _split3(a)
    return _dot_nt(ones, hi) + _dot_nt(ones, mid) + _dot_nt(ones, lo)


def _softplus(z):
    return jnp.maximum(z, 0.0) + jnp.log1p(jnp.exp(-jnp.abs(z)))


def _log_sigmoid(z):
    return jnp.minimum(z, 0.0) - jnp.log1p(jnp.exp(-jnp.abs(z)))


def _params(n_axes):
    return pltpu.CompilerParams(dimension_semantics=("arbitrary",) * n_axes,
                                vmem_limit_bytes=VMEM_LIMIT)


def _resident(shape):
    nd = len(shape)
    return pl.BlockSpec(shape, lambda *_: (0,) * nd, pipeline_mode=pl.Buffered(1))


def _rope_rows(ref_head, rows, cos, sin):
    x1 = rows[0:ROPE_HALF]
    x2 = rows[ROPE_HALF:ROPE_DIM]
    ref_head[0:ROPE_HALF] = x1 * cos - x2 * sin
    ref_head[ROPE_HALF:ROPE_DIM] = x2 * cos + x1 * sin
    ref_head[ROPE_DIM:] = rows[ROPE_DIM:]


def _inproj_kernel(x_ref, wt_ref, wk_ref, wf_ref, wfr_ref, wg_ref, bf_ref, bfr_ref, cq_ref, sq_ref, cos_ref, sin_ref,
                   qt_ref, krow_ref, kvsb_ref, kvmb_ref, kvfx_ref, logf_ref, cbc_ref, gate_ref, carry_ref, *, tm):
    t = pl.program_id(1)
    x = x_ref[0]
    xb, xlo = _split2(x)
    cos = cos_ref[...]
    sin = sin_ref[...]

    pt = _dot_nt(wt_ref[...], xb)
    head = lambda r: pt[r:r + HEAD_DIM]
    for h in range(H_SB):
        qt_ref[0, h] = head(h * HEAD_DIM)
    for h in range(H_MB):
        _rope_rows(qt_ref.at[0, H_SB + h], head(W_SB + h * HEAD_DIM), cos, sin)
    for h in range(H_FX):
        qt_ref[0, H_SB + H_MB + h] = head(W_SB + W_MB + h * HEAD_DIM)
    base = W_ALL
    for h in range(2 * H_SB):
        kvsb_ref[0, h // H_SB, h % H_SB] = head(base + h * HEAD_DIM)
    base += 2 * W_SB
    for h in range(H_MB):
        _rope_rows(kvmb_ref.at[0, 0, h], head(base + h * HEAD_DIM), cos, sin)
        kvmb_ref[0, 1, h] = head(base + W_MB + h * HEAD_DIM)
    base += 2 * W_MB
    for h in range(2 * H_FX):
        kvfx_ref[0, h // H_FX, h % H_FX] = head(base + h * HEAD_DIM)

    kr = _dot(xb, wk_ref[...])
    k_mb = kr[:, W_SB:W_SB + W_MB] * cq_ref[...] + kr[:, W_ALL:] * sq_ref[...]
    for h in range(H_SB):
        krow_ref[0, h] = kr[:, h * HEAD_DIM:(h + 1) * HEAD_DIM].astype(bf16)
    for h in range(H_MB):
        krow_ref[0, H_SB + h] = k_mb[:, h * HEAD_DIM:(h + 1) * HEAD_DIM].astype(bf16)
    for h in range(H_FX):
        o = W_SB + W_MB + h * HEAD_DIM
        krow_ref[0, H_SB + H_MB + h] = kr[:, o:o + HEAD_DIM].astype(bf16)

    r1 = _dot_nt(wf_ref[...], xb)
    r2 = _dot_nt(wf_ref[0:SUBLANES], xlo)
    logf_ref[0] = _log_sigmoid(r1[0:SUBLANES] + r1[SUBLANES:] + r2 + bf_ref[...])

    f1 = _dot(xb, wfr_ref[...])
    f2 = _dot(xlo, wfr_ref[:, 0:LANES])
    logf_rows = _log_sigmoid(f1[:, 0:LANES] + f1[:, LANES:] + f2 + bfr_ref[...])

    @pl.when(t == 0)
    def _():
        carry_ref[...] = jnp.zeros_like(carry_ref)
    row = lax.broadcasted_iota(jnp.int32, (tm, tm), 0)
    col = lax.broadcasted_iota(jnp.int32, (tm, tm), 1)
    c_rows = _dot3_left((col <= row).astype(bf16), logf_rows) + carry_ref[...]
    carry_ref[...] = c_rows[tm - 1:tm]
    src = lax.broadcasted_iota(jnp.int32, (LANES, LANES), 0)
    for h in range(H_FX):
        cbc_ref[0, h] = _dot3(c_rows, (src == h).astype(bf16))

    g = _dot(xb, wg_ref[...])
    gate_ref[0] = (1.0 / (1.0 + jnp.exp(-g))).astype(gate_ref.dtype)


def _inproj(x, w, rope, tm):
    B, T, D = x.shape
    wt, wk, wf, wfr, wg, bfc, bfr = w
    cq, sq, cos_t, sin_t = rope
    tok = lambda b, t: (b, t, 0)
    out_shape = (
        jax.ShapeDtypeStruct((B, H_ALL, HEAD_DIM, T), f32),
        jax.ShapeDtypeStruct((B, H_ALL, T, HEAD_DIM), bf16),
        jax.ShapeDtypeStruct((B, 2, H_SB, HEAD_DIM, T), f32),
        jax.ShapeDtypeStruct((B, 2, H_MB, HEAD_DIM, T), f32),
        jax.ShapeDtypeStruct((B, 2, H_FX, HEAD_DIM, T), f32),
        jax.ShapeDtypeStruct((B, SUBLANES, T), f32),
        jax.ShapeDtypeStruct((B, H_FX, T, LANES), f32),
        jax.ShapeDtypeStruct((B, T, N_BRANCH * D), bf16),
    )
    kv_spec = lambda H: pl.BlockSpec((1, 2, H, HEAD_DIM, tm), lambda b, t: (b, 0, 0, 0, t))
    return pl.pallas_call(
        functools.partial(_inproj_kernel, tm=tm),
        grid=(B, T // tm),
        in_specs=[pl.BlockSpec((1, tm, D), tok)] + [_resident(a.shape) for a in w]
        + [pl.BlockSpec((tm, W_MB), lambda b, t: (t, 0)),
           pl.BlockSpec((tm, W_MB), lambda b, t: (t, 0)),
           pl.BlockSpec((ROPE_HALF, tm), lambda b, t: (0, t)),
           pl.BlockSpec((ROPE_HALF, tm), lambda b, t: (0, t))],
        out_specs=(pl.BlockSpec((1, H_ALL, HEAD_DIM, tm), lambda b, t: (b, 0, 0, t)),
                   pl.BlockSpec((1, H_ALL, tm, HEAD_DIM), lambda b, t: (b, 0, t, 0)),
                   kv_spec(H_SB), kv_spec(H_MB), kv_spec(H_FX),
                   pl.BlockSpec((1, SUBLANES, tm), lambda b, t: (b, 0, t)),
                   pl.BlockSpec((1, H_FX, tm, LANES), lambda b, t: (b, 0, t, 0)),
                   pl.BlockSpec((1, tm, N_BRANCH * D), tok)),
        out_shape=out_shape,
        scratch_shapes=[pltpu.VMEM((1, LANES), f32)],
        compiler_params=_params(2),
        name="inproj",
    )(x, *w, cq, sq, cos_t, sin_t)


def _tile_iotas(n):
    row = lax.broadcasted_iota(jnp.int32, (n, n), 0)
    col = lax.broadcasted_iota(jnp.int32, (n, n), 1)
    return row, col


def _sb_kernel(qt_ref, kr_ref, v_ref, o_ref, carry_ref, acc_ref, *, tq):
    i = pl.program_id(2)
    row, col = _tile_iotas(tq)
    later = (col > row).astype(bf16)
    strict = row < col

    def block(j, diag):
        start = pl.multiple_of(j * tq, tq)
        carry = [carry_ref[g] for g in range(SB_HEAD_GROUP)]
        acc = [acc_ref[g] for g in range(SB_HEAD_GROUP)]
        for g in range(SB_HEAD_GROUP):
            qtb = qt_ref[0, g].astype(bf16)
            zt = _dot(kr_ref[0, g, pl.ds(start, tq), :], qtb)
            l1m = jnp.minimum(-zt, 0.0) - jnp.log(1.0 + jnp.exp(jnp.minimum(zt, -zt)))
            if diag:
                l1m = jnp.where(strict, l1m, 0.0)
            hi, lo = _split2(l1m)
            after = _dot(later, hi) + _dot(later, lo) + carry[g]
            w = jnp.exp(zt + l1m + after)
            if diag:
                w = jnp.where(strict, w, 0.0)
            vt = v_ref[0, 0, g, :, pl.ds(start, tq)].astype(bf16)
            acc[g] = acc[g] + _dot(vt, w.astype(bf16))
            carry[g] = carry[g] + jnp.sum(l1m, axis=0, keepdims=True)
        for g in range(SB_HEAD_GROUP):
            acc_ref[g] = acc[g]
            carry_ref[g] = carry[g]

    acc_ref[...] = jnp.zeros_like(acc_ref)
    carry_ref[...] = jnp.zeros_like(carry_ref)
    block(i, True)

    def cond(s):
        j, top = s
        return jnp.logical_and(j >= 0, top > EXP_ZERO_BELOW)

    def body(s):
        j, _ = s
        block(j, False)
        return j - 1, jnp.max(carry_ref[...])

    lax.while_loop(cond, body, (i - 1, jnp.max(carry_ref[...])))
    for g in range(SB_HEAD_GROUP):
        o_ref[0, g] = jnp.transpose(acc_ref[g]).astype(o_ref.dtype)


def _softmax_init(m_ref, l_ref, acc_ref):
    m_ref[...] = jnp.full_like(m_ref, NEG)
    l_ref[...] = jnp.zeros_like(l_ref)
    acc_ref[...] = jnp.zeros_like(acc_ref)


def _softmax_step(st, vt, m_ref, l_ref, acc_ref):
    m_old = m_ref[...]
    m_new = jnp.maximum(m_old, jnp.max(st, axis=0, keepdims=True))
    alpha = jnp.exp(m_old - m_new)
    p = jnp.exp(st - m_new)
    l_ref[...] = alpha * l_ref[...] + jnp.sum(p, axis=0, keepdims=True)
    acc_ref[...] = alpha * acc_ref[...] + _dot(vt, p.astype(bf16))
    m_ref[...] = m_new


def _softmax_finish(o_ref, l_ref, acc_ref):
    o_ref[0, 0] = jnp.transpose(acc_ref[...] / l_ref[...]).astype(o_ref.dtype)


def _causal(chunk_start, query_start, tk, tq):
    row = lax.broadcasted_iota(jnp.int32, (tk, tq), 0)
    col = lax.broadcasted_iota(jnp.int32, (tk, tq), 1)
    return row - col <= query_start - chunk_start


def _fox_kernel(qt_ref, kr_ref, v_ref, cbc_ref, o_ref, kn_ref, m_ref, l_ref, acc_ref, *, tq, kb):
    i = pl.program_id(2)
    tk = kb * tq
    n_keys = kr_ref.shape[2]

    @pl.when(i == 0)
    def _():
        def body(n, top):
            k = kr_ref[0, 0, pl.ds(pl.multiple_of(n * NORM_ROWS, NORM_ROWS), NORM_ROWS), :].astype(f32)
            return jnp.maximum(top, jnp.max(jnp.sum(k * k, axis=1, keepdims=True), axis=0, keepdims=True))
        kn_ref[...] = jnp.sqrt(lax.fori_loop(0, n_keys // NORM_ROWS, body, jnp.zeros((1, 1), f32)))

    qt = qt_ref[0, 0]
    qtb = qt.astype(bf16)
    q_norm = jnp.sqrt(jnp.max(jnp.sum(qt * qt, axis=0, keepdims=True), axis=1, keepdims=True))
    qk_bound = 1.01 * q_norm * kn_ref[...] + 0.01

    c_base = cbc_ref[0, 0, pl.ds(pl.multiple_of(i * tq, tq), 1), :]

    def scores(c, diag):
        start = pl.multiple_of(c * tk, tk)
        st = _dot(kr_ref[0, 0, pl.ds(start, tk), :], qtb)
        cb = cbc_ref[0, 0, pl.ds(start, tk), :] - c_base
        st = st - jnp.concatenate([cb] * (tq // LANES), axis=1)
        if diag:
            st = jnp.where(_causal(start, i * tq, tk, tq), st, NEG)
        return st

    def absorb(st, c):
        start = pl.multiple_of(c * tk, tk)
        vt = v_ref[0, 0, 0, :, pl.ds(start, tk)].astype(bf16)
        _softmax_step(st, vt, m_ref, l_ref, acc_ref)

    def reaches(c):
        last = jnp.maximum((c + 1) * tk - 1, 0)
        c_last = cbc_ref[0, 0, pl.ds(last, 1), :] - c_base
        m_low = jnp.min(m_ref[...], axis=1, keepdims=True)
        return jnp.max(qk_bound - c_last - m_low) >= EXP_ZERO_BELOW

    _softmax_init(m_ref, l_ref, acc_ref)
    c0 = i // kb

    def cond(s):
        return s[1]

    def body(s):
        c, _, st = s
        nxt = scores(jnp.maximum(c - 1, 0), False)
        absorb(st, c)
        return c - 1, jnp.logical_and(c >= 1, reaches(c - 1)), nxt

    lax.while_loop(cond, body, (c0, True, scores(c0, True)))
    _softmax_finish(o_ref, l_ref, acc_ref)


def _top_blocks(gate, cand, idx, axis):
    n = gate.shape[axis]
    gate = jnp.where(cand, gate, NEG)
    chosen = jnp.zeros(gate.shape, f32)
    firsts = []
    for _ in range(MOBA_TOPK):
        top = jnp.max(gate, axis=axis, keepdims=True)
        first = jnp.min(jnp.where(gate == top, idx, n), axis=axis, keepdims=True)
        pick = idx == first
        chosen = jnp.where(jnp.logical_and(pick, cand), 1.0, chosen)
        gate = jnp.where(pick, -jnp.inf, gate)
        firsts.append(first)
    return chosen, firsts


def _moba_kernel(qt_ref, kr_ref, k_ref, v_ref, o_ref, km_ref, ch_ref, m_ref, l_ref, acc_ref, *, tq, kb, nb):
    i = pl.program_id(2)
    tk = kb * tq

    @pl.when(i == 0)
    def _():
        lane = lax.broadcasted_iota(jnp.int32, (HEAD_DIM, LANES), 1)

        def body(j, km):
            start = pl.multiple_of(j * tq, tq)
            blk = k_ref[0, 0, 0, :, pl.ds(start, tq)]
            mean = jnp.sum(blk, axis=1, keepdims=True) * (1.0 / tq)
            return jnp.where(lane == j, mean, km)

        km_ref[...] = jnp.transpose(lax.fori_loop(0, nb, body, jnp.zeros((HEAD_DIM, LANES), f32)))

    qt = qt_ref[0, 0]
    qtb = qt.astype(bf16)
    qh, ql = _split2(qt)
    kh, kl = _split2(km_ref[...])
    gate = _dot(kh, qh) + _dot(kh, ql) + _dot(kl, qh)
    blk = lax.broadcasted_iota(jnp.int32, (LANES, tq), 0)
    chosen, _ = _top_blocks(gate, blk < i, blk, 0)
    ch_ref[...] = jnp.where(blk == i, 1.0, chosen)

    def scores(c, diag):
        start = pl.multiple_of(c * tk, tk)
        st = _dot(kr_ref[0, 0, pl.ds(start, tk), :], qtb)
        picked = jnp.concatenate(
            [jnp.broadcast_to(ch_ref[pl.ds(c * kb + t, 1), :], (tq, tq)) for t in range(kb)], axis=0) > 0.0
        if diag:
            picked = jnp.logical_and(picked, _causal(start, i * tq, tk, tq))
        return jnp.where(picked, st, NEG)

    def absorb(st, c):
        start = pl.multiple_of(c * tk, tk)
        vt = v_ref[0, 0, 0, :, pl.ds(start, tk)].astype(bf16)
        _softmax_step(st, vt, m_ref, l_ref, acc_ref)

    _softmax_init(m_ref, l_ref, acc_ref)
    c0 = i // kb

    def body(n, st):
        nxt = scores(jnp.minimum(n, jnp.maximum(c0 - 1, 0)), False)
        absorb(st, jnp.where(n == 0, c0, n - 1))
        return nxt

    lax.fori_loop(0, c0 + 1, body, scores(c0, True))
    _softmax_finish(o_ref, l_ref, acc_ref)


def _softmax_scratch(tq):
    return [pltpu.VMEM((1, tq), f32), pltpu.VMEM((1, tq), f32), pltpu.VMEM((HEAD_DIM, tq), f32)]


def _prompt_attention(kernel, name, head_off, n_heads, operands, specs, scratch):
    qt = operands[0]
    B, _, hd, T = qt.shape
    return pl.pallas_call(
        kernel,
        grid=(B, n_heads, T // TQ),
        in_specs=[pl.BlockSpec((1, 1, hd, TQ), lambda b, h, i: (b, h + head_off, 0, i))] + specs,
        out_specs=pl.BlockSpec((1, 1, TQ, hd), lambda b, h, i: (b, h, i, 0)),
        out_shape=jax.ShapeDtypeStruct((B, n_heads, T, hd), bf16),
        scratch_shapes=scratch,
        compiler_params=_params(3),
        name=name,
    )(*operands)


def _kvt_spec(T, kv):
    return pl.BlockSpec((1, 1, 1, HEAD_DIM, T), lambda b, h, i: (b, kv, h, 0, 0))


def _krow_spec(T, head_off):
    return pl.BlockSpec((1, 1, T, HEAD_DIM), lambda b, h, i: (b, h + head_off, 0, 0))


def _sb_prompt(qt, krow, kvt):
    B, _, hd, T = qt.shape
    G = SB_HEAD_GROUP
    assert H_SB % G == 0
    return pl.pallas_call(
        functools.partial(_sb_kernel, tq=TQ),
        grid=(B, H_SB // G, T // TQ),
        in_specs=[pl.BlockSpec((1, G, hd, TQ), lambda b, h, i: (b, h, 0, i)),
                  pl.BlockSpec((1, G, T, hd), lambda b, h, i: (b, h, 0, 0)),
                  pl.BlockSpec((1, 1, G, hd, T), lambda b, h, i: (b, 1, h, 0, 0))],
        out_specs=pl.BlockSpec((1, G, TQ, hd), lambda b, h, i: (b, h, i, 0)),
        out_shape=jax.ShapeDtypeStruct((B, H_SB, T, hd), bf16),
        scratch_shapes=[pltpu.VMEM((G, 1, TQ), f32), pltpu.VMEM((G, hd, TQ), f32)],
        compiler_params=_params(3),
        name="sb_prompt",
    )(qt, krow, kvt)


def _moba_prompt(qt, krow, kvt):
    T = qt.shape[3]
    assert T % (MOBA_CHUNK_BLOCKS * TQ) == 0 and T // TQ <= LANES
    return _prompt_attention(functools.partial(_moba_kernel, tq=TQ, kb=MOBA_CHUNK_BLOCKS, nb=T // TQ),
                             "moba_prompt", H_SB, H_MB, (qt, krow, kvt, kvt),
                             [_krow_spec(T, H_SB), _kvt_spec(T, 0), _kvt_spec(T, 1)],
                             [pltpu.VMEM((LANES, HEAD_DIM), f32), pltpu.VMEM((LANES, TQ), f32)] + _softmax_scratch(TQ))


def _fox_prompt(qt, krow, kvt, cbc):
    T = qt.shape[3]
    assert T % (FOX_CHUNK_BLOCKS * TQ) == 0 and T % NORM_ROWS == 0
    return _prompt_attention(functools.partial(_fox_kernel, tq=TQ, kb=FOX_CHUNK_BLOCKS), "fox_prompt",
                             H_SB + H_MB, H_FX, (qt, krow, kvt, cbc),
                             [_krow_spec(T, H_SB + H_MB), _kvt_spec(T, 1),
                              pl.BlockSpec((1, 1, T, LANES), lambda b, h, i: (b, h, 0, 0))],
                             [pltpu.VMEM((1, 1), f32)] + _softmax_scratch(TQ))


def _layer_norm(x, g, b):
    mu = jnp.mean(x, axis=-1, keepdims=True)
    xc = x - mu
    var = jnp.mean(xc * xc, axis=-1, keepdims=True)
    return xc * lax.rsqrt(var + LN_EPS) * g + b


def _post_kernel(x_ref, osb_ref, omb_ref, ofx_ref, gate_ref, wsb_ref, wmb_ref, wfx_ref, wo_ref, wup_ref, wdn_ref,
                 g1_ref, b1_ref, g2_ref, b2_ref, y_ref):
    x = x_ref[0]
    merged = None
    for br, (o_ref, w_ref) in enumerate(((osb_ref, wsb_ref), (omb_ref, wmb_ref), (ofx_ref, wfx_ref))):
        pb = None
        for h in range(o_ref.shape[1]):
            t = _dot(o_ref[0, h], w_ref[h])
            pb = t if pb is None else pb + t
        gb = gate_ref[0, :, br * D_MODEL:(br + 1) * D_MODEL].astype(f32)
        merged = gb * pb if merged is None else merged + gb * pb
    u = _dot(merged.astype(bf16), wo_ref[...])
    x1 = _layer_norm(DN_ALPHA * x + u, g1_ref[...], b1_ref[...])
    x1b = x1.astype(bf16)
    acc = None
    for f in range(D_FF // FF_CHUNK):
        hm = jnp.maximum(_dot(x1b, wup_ref[:, f * FF_CHUNK:(f + 1) * FF_CHUNK]), 0.0)
        t = _dot((hm * hm).astype(bf16), wdn_ref[f * FF_CHUNK:(f + 1) * FF_CHUNK, :])
        acc = t if acc is None else acc + t
    y_ref[0] = _layer_norm(DN_ALPHA * x1 + acc, g2_ref[...], b2_ref[...])


def _post(x, o_sb, o_mb, o_fx, gates, w, tm):
    B, T, D = x.shape
    tok = lambda b, t: (b, t, 0)
    o_spec = lambda H: pl.BlockSpec((1, H, tm, HEAD_DIM), lambda b, t: (b, 0, t, 0))
    return pl.pallas_call(
        _post_kernel,
        grid=(B, T // tm),
        in_specs=[pl.BlockSpec((1, tm, D), tok), o_spec(H_SB), o_spec(H_MB), o_spec(H_FX),
                  pl.BlockSpec((1, tm, N_BRANCH * D), tok)] + [_resident(a.shape) for a in w],
        out_specs=pl.BlockSpec((1, tm, D), tok),
        out_shape=jax.ShapeDtypeStruct((B, T, D), f32),
        compiler_params=_params(2),
        name="post",
    )(x, o_sb, o_mb, o_fx, gates, *w)


def _page_scores(z_ref, k_page, qb_ref, n_heads):
    for h in range(n_heads):
        z_ref[h:h + 1, :] = jnp.sum(k_page[h] * qb_ref[0, h], axis=0, keepdims=True)


def _later_matrix():
    row, col = _tile_iotas(PAGE)
    return (row > col).astype(bf16)


def _finish_heads(o_ref, acc_ref, n_heads, scale=None):
    o_ref[0] = jnp.zeros(o_ref.shape[1:], f32)
    for h in range(n_heads):
        r = _lane_sum_rows(acc_ref[h])[0:1]
        if scale is not None:
            r = r * scale[h:h + 1, 0:HEAD_DIM]
        o_ref[0, h:h + 1, :] = r


def _sb_pages(pages, qb_ref, z_ref, acc_ref, carry_ref, n_heads):
    later = _later_matrix()
    for page in pages:
        @pl.when(jnp.max(carry_ref[0:n_heads]) > EXP_ZERO_BELOW)
        def _():
            _page_scores(z_ref, page.at[0], qb_ref, n_heads)
            z = z_ref[...]
            l1m = -_softplus(z)
            hi, lo = _split2(l1m)
            after = _dot(hi, later) + _dot(lo, later) + carry_ref[...]
            w = jnp.exp(z + l1m + after)
            for h in range(n_heads):
                acc_ref[h] += page[1, h] * w[h:h + 1, :]
            carry_ref[...] += jnp.sum(l1m, axis=1, keepdims=True)


def _sb_dec_head_kernel(pt_ref, qb_ref, *refs, n_heads):
    pages = refs[:PAGES_PER_STEP]
    acc_out, carry_out, z_ref, carry_ref = refs[PAGES_PER_STEP:]
    z_ref[...] = jnp.zeros_like(z_ref)
    carry_ref[...] = jnp.zeros_like(carry_ref)
    acc_out[...] = jnp.zeros_like(acc_out)
    _sb_pages(pages, qb_ref, z_ref, acc_out.at[0], carry_ref, n_heads)
    carry_out[0] = jnp.broadcast_to(carry_ref[...], (SUBLANES, LANES))


def _sb_dec_tail_kernel(pt_ref, done_ref, qb_ref, acc_in, carry_in, *refs, n_heads):
    pages = refs[:PAGES_PER_STEP]
    o_ref, z_ref, acc_ref, carry_ref = refs[PAGES_PER_STEP:]
    b = pl.program_id(0)
    c = pl.program_id(1)

    @pl.when(c == 0)
    def _():
        z_ref[...] = jnp.zeros_like(z_ref)
        acc_ref[...] = acc_in[0]
        carry_ref[...] = carry_in[0][:, 0:1]

    @pl.when(done_ref[b] == 0)
    def _():
        _sb_pages(pages, qb_ref, z_ref, acc_ref, carry_ref, n_heads)

    @pl.when(c == pl.num_programs(1) - 1)
    def _():
        _finish_heads(o_ref, acc_ref, n_heads)


def _fox_dec_kernel(pt_ref, qb_ref, kb_ref, vb_ref, lfn_ref, *refs, n_heads):
    pages = refs[:PAGES_PER_STEP]
    lf_pages = refs[PAGES_PER_STEP:2 * PAGES_PER_STEP]
    o_ref, z_ref, lf_ref, m_ref, l_ref, acc_ref, carry_ref = refs[2 * PAGES_PER_STEP:]
    c = pl.program_id(1)

    @pl.when(c == 0)
    def _():
        z_ref[...] = jnp.zeros_like(z_ref)
        lf_ref[...] = jnp.zeros_like(lf_ref)
        carry_ref[...] = jnp.zeros_like(carry_ref)
        _page_scores(z_ref, kb_ref.at[0], qb_ref, n_heads)
        m_ref[...] = z_ref[...]
        l_ref[...] = jnp.ones_like(l_ref)
        lane0 = lax.broadcasted_iota(jnp.int32, (HEAD_DIM, LANES), 1) == 0
        for h in range(n_heads):
            acc_ref[h] = jnp.where(lane0, vb_ref[0, h], 0.0)

    later = _later_matrix()
    for page, lf_page in zip(pages, lf_pages):
        _page_scores(z_ref, page.at[0], qb_ref, n_heads)
        for h in range(n_heads):
            lf_ref[h:h + 1, :] = lf_page[h]
        lf = lf_ref[...]
        s = z_ref[...] + _dot3(lf, later) + carry_ref[...] + lfn_ref[0]
        m_old = m_ref[...]
        m_new = jnp.maximum(m_old, jnp.max(s, axis=1, keepdims=True))
        alpha = jnp.exp(m_old - m_new)
        p = jnp.exp(s - m_new)
        l_ref[...] = alpha * l_ref[...] + jnp.sum(p, axis=1, keepdims=True)
        m_ref[...] = m_new
        for h in range(n_heads):
            acc_ref[h] = acc_ref[h] * alpha[h:h + 1, :] + page[1, h] * p[h:h + 1, :]
        carry_ref[...] += jnp.sum(lf, axis=1, keepdims=True)

    @pl.when(c == pl.num_programs(1) - 1)
    def _():
        _finish_heads(o_ref, acc_ref, n_heads, scale=1.0 / l_ref[...])


def _cache_view(cache):
    return jnp.transpose(cache, (0, 1, 3, 4, 5, 2))


def _bcast_spec(n_heads):
    return pl.BlockSpec((1, n_heads, HEAD_DIM, LANES), lambda b, c, pt: (b, 0, 0, 0))


def _page_specs(layer, n_heads, n_pages, newest_first, kv=None):
    specs = []
    for i in range(PAGES_PER_STEP):
        def page_of(b, c, pt, i=i):
            lp = c * PAGES_PER_STEP + i
            return pt[b, n_pages - 1 - lp] if newest_first else pt[b, lp]
        if kv is None:
            specs.append(pl.BlockSpec((None, None, 2, n_heads, HEAD_DIM, PAGE),
                                      lambda b, c, pt, f=page_of: (layer, f(b, c, pt), 0, 0, 0, 0)))
        else:
            specs.append(pl.BlockSpec((None, None, None, n_heads, HEAD_DIM, PAGE),
                                      lambda b, c, pt, f=page_of: (layer, f(b, c, pt), kv, 0, 0, 0)))
    return specs


def _dec_out(nseq):
    return (pl.BlockSpec((1, SUBLANES, HEAD_DIM), lambda b, c, pt: (b, 0, 0)),
            jax.ShapeDtypeStruct((nseq, SUBLANES, HEAD_DIM), f32))


def _sb_decode(cache_t, layer, page_table, qb):
    nseq, n_pages = page_table.shape
    n_heads = cache_t.shape[3]
    steps = n_pages // PAGES_PER_STEP
    assert steps >= 2
    state = pl.BlockSpec((1, n_heads, HEAD_DIM, LANES), lambda b, c, *_: (b, 0, 0, 0))
    row8 = pl.BlockSpec((1, SUBLANES, LANES), lambda b, c, *_: (b, 0, 0))
    z_scratch = pltpu.VMEM((SUBLANES, PAGE), f32)
    carry_scratch = pltpu.VMEM((SUBLANES, 1), f32)
    gs = pltpu.PrefetchScalarGridSpec(
        num_scalar_prefetch=1, grid=(nseq, 1),
        in_specs=[_bcast_spec(n_heads)] + _page_specs(layer, n_heads, n_pages, True),
        out_specs=(state, row8), scratch_shapes=[z_scratch, carry_scratch])
    acc, carry = pl.pallas_call(
        functools.partial(_sb_dec_head_kernel, n_heads=n_heads), grid_spec=gs,
        out_shape=(jax.ShapeDtypeStruct((nseq, n_heads, HEAD_DIM, LANES), f32),
                   jax.ShapeDtypeStruct((nseq, SUBLANES, LANES), f32)),
        compiler_params=_params(2), name="sb_decode_head",
    )(page_table, qb, *([cache_t] * PAGES_PER_STEP))

    done = (jnp.max(carry[:, :n_heads, 0], axis=1) <= EXP_ZERO_BELOW).astype(jnp.int32)
    specs = []
    for i in range(PAGES_PER_STEP):
        def page_of(b, c, pt, done, i=i):
            lp = (c + 1) * PAGES_PER_STEP + i
            return jnp.where(done[b] == 0, pt[b, n_pages - 1 - lp], 0)
        specs.append(pl.BlockSpec((None, None, 2, n_heads, HEAD_DIM, PAGE),
                                  lambda b, c, pt, done, f=page_of: (layer, f(b, c, pt, done), 0, 0, 0, 0)))
    out_spec, out_shape = _dec_out(nseq)
    gs = pltpu.PrefetchScalarGridSpec(
        num_scalar_prefetch=2, grid=(nseq, steps - 1),
        in_specs=[pl.BlockSpec((1, n_heads, HEAD_DIM, LANES), lambda b, c, *_: (b, 0, 0, 0)), state, row8] + specs,
        out_specs=pl.BlockSpec((1, SUBLANES, HEAD_DIM), lambda b, c, *_: (b, 0, 0)),
        scratch_shapes=[z_scratch, pltpu.VMEM((n_heads, HEAD_DIM, PAGE), f32), carry_scratch])
    return pl.pallas_call(
        functools.partial(_sb_dec_tail_kernel, n_heads=n_heads), grid_spec=gs, out_shape=out_shape,
        compiler_params=_params(2), name="sb_decode_tail",
    )(page_table, done, qb, acc, carry, *([cache_t] * PAGES_PER_STEP))


def _fox_decode(cache_t, logf_t, layer, page_table, qb, kb, vb, lfn):
    nseq, n_pages = page_table.shape
    n_heads = cache_t.shape[3]
    out_spec, out_shape = _dec_out(nseq)
    lf_specs = []
    for i in range(PAGES_PER_STEP):
        lf_specs.append(pl.BlockSpec(
            (None, n_heads, None, 1, PAGE),
            lambda b, c, pt, i=i: (layer, 0, pt[b, n_pages - 1 - (c * PAGES_PER_STEP + i)], 0, 0)))
    gs = pltpu.PrefetchScalarGridSpec(
        num_scalar_prefetch=1, grid=(nseq, n_pages // PAGES_PER_STEP),
        in_specs=[_bcast_spec(n_heads)] * 3 + [pl.BlockSpec((1, SUBLANES, LANES), lambda b, c, pt: (b, 0, 0))]
        + _page_specs(layer, n_heads, n_pages, True) + lf_specs,
        out_specs=out_spec,
        scratch_shapes=[pltpu.VMEM((SUBLANES, PAGE), f32), pltpu.VMEM((SUBLANES, PAGE), f32),
                        pltpu.VMEM((SUBLANES, PAGE), f32), pltpu.VMEM((SUBLANES, PAGE), f32),
                        pltpu.VMEM((n_heads, HEAD_DIM, PAGE), f32), pltpu.VMEM((SUBLANES, 1), f32)])
    return pl.pallas_call(functools.partial(_fox_dec_kernel, n_heads=n_heads), grid_spec=gs, out_shape=out_shape,
                          compiler_params=_params(2), name="fox_decode",
                          )(page_table, qb, kb, vb, lfn, *([cache_t] * PAGES_PER_STEP),
                            *([logf_t] * PAGES_PER_STEP))


def _mb_scores_kernel(pt_ref, qb_ref, kb_ref, *refs, n_heads):
    pages = refs[:PAGES_PER_STEP]
    z_out, bsum_ref, zself_ref, z_ref = refs[PAGES_PER_STEP:]
    c = pl.program_id(1)

    @pl.when(c == 0)
    def _():
        z_ref[...] = jnp.zeros_like(z_ref)
        bsum_ref[...] = jnp.zeros_like(bsum_ref)
        _page_scores(z_ref, kb_ref.at[0], qb_ref, n_heads)
        zself_ref[0] = z_ref[...]

    lane = lax.broadcasted_iota(jnp.int32, (SUBLANES, LANES), 1)
    for i, page in enumerate(pages):
        _page_scores(z_ref, page, qb_ref, n_heads)
        z = z_ref[...]
        z_out[0, :, i * PAGE:(i + 1) * PAGE] = z
        blk = (c * PAGES_PER_STEP + i) // PAGES_PER_BLOCK
        bsum_ref[0] += jnp.where(lane == blk, jnp.sum(z, axis=1, keepdims=True), 0.0)


def _mb_select_kernel(z_ref, bsum_ref, zself_ref, p_ref, pself_ref, idx_ref, *, n_blocks):
    gate = bsum_ref[0] * (1.0 / MOBA_BLOCK)
    lane = lax.broadcasted_iota(jnp.int32, (SUBLANES, LANES), 1)
    _, firsts = _top_blocks(gate, lane < n_blocks, lane, 1)
    idx = jnp.zeros((SUBLANES, LANES), jnp.int32)
    for r, first in enumerate(firsts):
        idx = jnp.where(lane == r, first, idx)
    idx_ref[0] = idx
    n_keys = z_ref.shape[2]
    blk_of_key = lax.broadcasted_iota(jnp.int32, (SUBLANES, n_keys), 1) // MOBA_BLOCK
    picked = None
    for first in firsts:
        hit = jnp.logical_and(blk_of_key == first, first < n_blocks)
        picked = hit if picked is None else jnp.logical_or(picked, hit)
    s = jnp.where(picked, z_ref[0], NEG)
    zs = zself_ref[0]
    m = jnp.maximum(jnp.max(s, axis=1, keepdims=True), zs[:, 0:1])
    p = jnp.exp(s - m)
    ps = jnp.exp(zs - m)
    inv = 1.0 / (jnp.sum(p, axis=1, keepdims=True) + ps[:, 0:1])
    p_ref[0] = p * inv
    pself_ref[0] = ps * inv


N_PICKED_PAGES = MOBA_TOPK * PAGES_PER_BLOCK


def _mb_pv_kernel(pt_ref, idx_ref, *refs):
    v_pages = refs[:N_PICKED_PAGES]
    p_pages = refs[N_PICKED_PAGES:2 * N_PICKED_PAGES]
    pself_ref, vb_ref, o_ref = refs[2 * N_PICKED_PAGES:]
    h = pl.program_id(1)

    @pl.when(h == 0)
    def _():
        o_ref[...] = jnp.zeros_like(o_ref)

    lane0 = lax.broadcasted_iota(jnp.int32, (HEAD_DIM, LANES), 1) == 0
    acc = jnp.where(lane0, vb_ref[0, h] * pself_ref[0, pl.ds(h, 1), :], 0.0)
    for v_page, p_page in zip(v_pages, p_pages):
        acc = acc + v_page[...] * p_page[0, pl.ds(h, 1), :]
    o_ref[0, pl.ds(h, 1), :] = _lane_sum_rows(acc)[0:1]


def _moba_decode(cache_t, layer, page_table, qb, kb, vb):
    nseq, n_pages = page_table.shape
    n_heads = cache_t.shape[3]
    n_keys = n_pages * PAGE
    n_blocks = n_keys // MOBA_BLOCK
    assert MOBA_TOPK <= n_blocks <= LANES
    row8 = pl.BlockSpec((1, SUBLANES, LANES), lambda b, c, pt: (b, 0, 0))
    gs = pltpu.PrefetchScalarGridSpec(
        num_scalar_prefetch=1, grid=(nseq, n_pages // PAGES_PER_STEP),
        in_specs=[_bcast_spec(n_heads)] * 2 + _page_specs(layer, n_heads, n_pages, False, kv=0),
        out_specs=(pl.BlockSpec((1, SUBLANES, PAGES_PER_STEP * PAGE), lambda b, c, pt: (b, 0, c)), row8, row8),
        scratch_shapes=[pltpu.VMEM((SUBLANES, PAGE), f32)])
    z, bsum, zself = pl.pallas_call(
        functools.partial(_mb_scores_kernel, n_heads=n_heads), grid_spec=gs,
        out_shape=(jax.ShapeDtypeStruct((nseq, SUBLANES, n_keys), f32),
                   jax.ShapeDtypeStruct((nseq, SUBLANES, LANES), f32),
                   jax.ShapeDtypeStruct((nseq, SUBLANES, LANES), f32)),
        compiler_params=_params(2), name="moba_decode_scores",
    )(page_table, qb, kb, *([cache_t] * PAGES_PER_STEP))

    keys = pl.BlockSpec((1, SUBLANES, n_keys), lambda b: (b, 0, 0))
    small = pl.BlockSpec((1, SUBLANES, LANES), lambda b: (b, 0, 0))
    p, pself, idx = pl.pallas_call(
        functools.partial(_mb_select_kernel, n_blocks=n_blocks),
        grid=(nseq,), in_specs=[keys, small, small], out_specs=(keys, small, small),
        out_shape=(jax.ShapeDtypeStruct((nseq, SUBLANES, n_keys), f32),
                   jax.ShapeDtypeStruct((nseq, SUBLANES, LANES), f32),
                   jax.ShapeDtypeStruct((nseq, SUBLANES, LANES), jnp.int32)),
        compiler_params=_params(1), name="moba_decode_select",
    )(z, bsum, zself)

    idx_flat = idx[:, :, :MOBA_TOPK].reshape(nseq, SUBLANES * MOBA_TOPK)

    def page_no(b, h, k, idx):
        return idx[b, h * MOBA_TOPK + k // PAGES_PER_BLOCK] * PAGES_PER_BLOCK + k % PAGES_PER_BLOCK

    v_specs = [pl.BlockSpec((None, None, None, None, HEAD_DIM, PAGE),
                            lambda b, h, pt, idx, k=k: (layer, pt[b, page_no(b, h, k, idx)], 1, h, 0, 0))
               for k in range(N_PICKED_PAGES)]
    p_specs = [pl.BlockSpec((1, SUBLANES, PAGE), lambda b, h, pt, idx, k=k: (b, 0, page_no(b, h, k, idx)))
               for k in range(N_PICKED_PAGES)]
    gs = pltpu.PrefetchScalarGridSpec(
        num_scalar_prefetch=2, grid=(nseq, n_heads),
        in_specs=v_specs + p_specs
        + [pl.BlockSpec((1, SUBLANES, LANES), lambda b, h, pt, idx: (b, 0, 0)),
           pl.BlockSpec((1, n_heads, HEAD_DIM, LANES), lambda b, h, pt, idx: (b, 0, 0, 0))],
        out_specs=pl.BlockSpec((1, SUBLANES, HEAD_DIM), lambda b, h, pt, idx: (b, 0, 0)))
    return pl.pallas_call(
        _mb_pv_kernel, grid_spec=gs, out_shape=jax.ShapeDtypeStruct((nseq, SUBLANES, HEAD_DIM), f32),
        compiler_params=_params(2), name="moba_decode_pv",
    )(page_table, idx_flat, *([cache_t] * N_PICKED_PAGES), *([p] * N_PICKED_PAGES), pself, vb)


def _rope_tables(pos):
    inv_freq = ROPE_THETA ** (-jnp.arange(ROPE_HALF, dtype=f32) / ROPE_HALF)
    ang = pos.astype(f32)[:, None] * inv_freq[None, :]
    cos, sin = jnp.cos(ang), jnp.sin(ang)
    n = pos.shape[0]
    rest = HEAD_DIM - ROPE_DIM
    cq = jnp.tile(jnp.concatenate([cos, cos, jnp.ones((n, rest), f32)], axis=1), (1, H_MB))
    sq = jnp.tile(jnp.concatenate([sin, sin, jnp.zeros((n, rest), f32)], axis=1), (1, H_MB))
    return cq, sq, cos.T, sin.T


def _hi_lo(w):
    hi = w.astype(bf16)
    return hi, (w - hi.astype(f32)).astype(bf16)


def _layer_weights(w_in, b_forget):
    scale = HEAD_DIM ** -0.5
    edges = [0]
    for wdt in (W_SB,) * 3 + (W_MB,) * 3 + (W_FX,) * 3 + (H_FX,):
        edges.append(edges[-1] + wdt)
    seg = [w_in[:, a:b] for a, b in zip(edges[:-1], edges[1:])]
    q_sb, k_sb, v_sb, q_mb, k_mb, v_mb, q_fx, k_fx, v_fx, w_f = seg
    w_g = w_in[:, edges[-1]:]
    km = k_mb.reshape(D_MODEL, H_MB, HEAD_DIM)
    k_sw = jnp.concatenate([-km[..., ROPE_HALF:ROPE_DIM], km[..., :ROPE_HALF],
                            jnp.zeros((D_MODEL, H_MB, HEAD_DIM - ROPE_DIM), f32)], axis=-1).reshape(D_MODEL, W_MB)
    wt = jnp.concatenate([q_sb * scale, q_mb * scale, q_fx * scale,
                          k_sb, v_sb, k_mb, v_mb, k_fx, v_fx], axis=1).T.astype(bf16)
    wk = jnp.concatenate([k_sb, k_mb, k_fx, k_sw], axis=1).astype(bf16)
    wf_hi, wf_lo = _hi_lo(jnp.pad(w_f, ((0, 0), (0, SUBLANES - H_FX))).T)
    wf = jnp.concatenate([wf_hi, wf_lo], axis=0)
    wfr_hi, wfr_lo = _hi_lo(jnp.pad(w_f, ((0, 0), (0, LANES - H_FX))))
    wfr = jnp.concatenate([wfr_hi, wfr_lo], axis=1)
    bfc = jnp.pad(b_forget, (0, SUBLANES - H_FX)).reshape(SUBLANES, 1)
    bfr = jnp.pad(b_forget, (0, LANES - H_FX)).reshape(1, LANES)
    return wt, wk, wf, wfr, w_g.astype(bf16), bfc, bfr


def _post_weights(w_br_sb, w_br_mb, w_br_fx, w_o, ln1_g, ln1_b, w_up, w_down, ln2_g, ln2_b):
    heads = lambda w: w.reshape(-1, HEAD_DIM, D_MODEL).astype(bf16)
    vec = lambda v: v.reshape(1, D_MODEL)
    return (heads(w_br_sb), heads(w_br_mb), heads(w_br_fx), w_o.astype(bf16), w_up.astype(bf16),
            w_down.astype(bf16), vec(ln1_g), vec(ln1_b), vec(ln2_g), vec(ln2_b))


def _lane_bcast(a):
    return jnp.broadcast_to(a[..., None], a.shape + (LANES,))


def kernel(x_prompt, x_sample, cache_kv_sb, cache_kv_moba, cache_kv_fox, cache_logf_fox, page_table,
           w_in, b_forget, w_br_sb, w_br_moba, w_br_fox, w_o, ln1_g, ln1_b, w_up, w_down, ln2_g, ln2_b):
    B, T, D = x_prompt.shape
    nseq = x_sample.shape[0]
    n_pages = page_table.shape[1]
    past_len = n_pages * PAGE

    rope_p = _rope_tables(jnp.arange(T, dtype=jnp.int32))
    rope_s = _rope_tables(jnp.full((nseq,), past_len, jnp.int32))
    sb_t, mb_t, fx_t = _cache_view(cache_kv_sb), _cache_view(cache_kv_moba), _cache_view(cache_kv_fox)
    pool = cache_logf_fox.shape[1]
    logf_t = jnp.transpose(cache_logf_fox, (0, 3, 1, 2)).reshape(DEPTH, H_FX, pool, 1, PAGE)

    yp = x_prompt
    ys = x_sample.reshape(1, nseq, D)
    rows_p, rows_s = [], []
    for l in range(DEPTH):
        w1 = _layer_weights(w_in[l], b_forget[l])
        w2 = _post_weights(w_br_sb[l], w_br_moba[l], w_br_fox[l], w_o[l], ln1_g[l], ln1_b[l],
                           w_up[l], w_down[l], ln2_g[l], ln2_b[l])

        qt, krow, kvsb, kvmb, kvfx, logf, cbc, gates = _inproj(yp, w1, rope_p, TM_INPROJ)
        o_sb = _sb_prompt(qt, krow, kvsb)
        o_mb = _moba_prompt(qt, krow, kvmb)
        o_fx = _fox_prompt(qt, krow, kvfx, cbc)
        yp = _post(yp, o_sb, o_mb, o_fx, gates, w2, TM_POST)
        rows_p.append((kvsb, kvmb, kvfx, logf))

        qt, _, kvsb, kvmb, kvfx, logf, _, gates = _inproj(ys, w1, rope_s, nseq)
        qd = _lane_bcast(jnp.transpose(qt[0], (2, 0, 1)))
        new = lambda kvt, j: _lane_bcast(jnp.transpose(kvt[0, j], (2, 0, 1)))
        lfn = _lane_bcast(logf[0].T)
        o_sb = _sb_decode(sb_t, l, page_table, qd[:, :H_SB])
        o_mb = _moba_decode(mb_t, l, page_table, qd[:, H_SB:H_SB + H_MB], new(kvmb, 0), new(kvmb, 1))
        o_fx = _fox_decode(fx_t, logf_t, l, page_table, qd[:, H_SB + H_MB:], new(kvfx, 0), new(kvfx, 1), lfn)
        heads = lambda o, H: jnp.transpose(o[:, :H], (1, 0, 2))[None].astype(bf16)
        ys = _post(ys, heads(o_sb, H_SB), heads(o_mb, H_MB), heads(o_fx, H_FX), gates, w2, nseq)
        rows_s.append((kvsb, kvmb, kvfx, logf))

    def kv_out(rows, j):
        return jnp.transpose(jnp.stack([r[j] for r in rows]), (0, 1, 5, 2, 3, 4))

    def logf_out(rows):
        return jnp.transpose(jnp.stack([r[3] for r in rows])[:, :, :H_FX], (0, 1, 3, 2))

    def sample(a):
        return jnp.swapaxes(a, 1, 2)

    return (yp, ys.reshape(nseq, 1, D),
            kv_out(rows_p, 0), kv_out(rows_p, 1), kv_out(rows_p, 2), logf_out(rows_p),
            sample(kv_out(rows_s, 0)), sample(kv_out(rows_s, 1)), sample(kv_out(rows_s, 2)),
            sample(logf_out(rows_s)))
```

```python
import functools

import jax
import jax.numpy as jnp
from jax import lax
from jax.experimental import pallas as pl
from jax.experimental.pallas import tpu as pltpu

bf16 = jnp.bfloat16
f32 = jnp.float32

D_MODEL = 1024
HEAD_DIM = 64
H_SB, H_MB, H_FX = 6, 5, 5
H_ALL = H_SB + H_MB + H_FX
W_SB, W_MB, W_FX = H_SB * HEAD_DIM, H_MB * HEAD_DIM, H_FX * HEAD_DIM
W_ALL = W_SB + W_MB + W_FX
N_BRANCH = 3
MOBA_BLOCK = 256
MOBA_TOPK = 3
ROPE_DIM = HEAD_DIM // 4
ROPE_HALF = ROPE_DIM // 2
ROPE_THETA = 500000.0
D_FF = 4 * D_MODEL
DEPTH = 2
DN_ALPHA = (2 * DEPTH) ** 0.25
LN_EPS = 1e-5
PAGE = 128
SUBLANES = 8
LANES = 128
VMEM_LIMIT = 56 << 20
NEG = -1e30
LOG2E = 1.4426950408889634
FIXED_STABILISER_LIMIT = 40.0
EXP_ZERO_BELOW = -104.0
TM_INPROJ = 256
TM_POST = 512
FF_CHUNK = 1024
TQ = MOBA_BLOCK
SB_HEAD_GROUP = 2
FOX_CHUNK_BLOCKS = 2
MOBA_CHUNK_BLOCKS = 4
NORM_ROWS = 1024
PAGES_PER_STEP = 32
PAGES_PER_BLOCK = MOBA_BLOCK // PAGE

NT_DIMS = (((1,), (1,)), ((), ()))


def _dot(a, b):
    return jnp.dot(a, b, preferred_element_type=f32)


def _dot_nt(a, b):
    return lax.dot_general(a, b, NT_DIMS, preferred_element_type=f32)


def _split2(x):
    hi = x.astype(bf16)
    lo = (x - hi.astype(f32)).astype(bf16)
    return hi, lo


def _split3(x):
    hi = x.astype(bf16)
    r = x - hi.astype(f32)
    mid = r.astype(bf16)
    lo = (r - mid.astype(f32)).astype(bf16)
    return hi, mid, lo


def _dot3(x, m):
    hi, mid, lo = _split3(x)
    return _dot(hi, m) + _dot(mid, m) + _dot(lo, m)


def _dot3_left(m, x):
    hi, mid, lo = _split3(x)
    return _dot(m, hi) + _dot(m, mid) + _dot(m, lo)


def _lane_sum_rows(a):
    ones = jnp.ones((SUBLANES, a.shape[1]), bf16)
    hi, mid, lo = _split3(a)
    return _dot_nt(ones, hi) + _dot_nt(ones, mid) + _dot_nt(ones, lo)


def _softplus(z):
    return jnp.maximum(z, 0.0) + jnp.log1p(jnp.exp(-jnp.abs(z)))


def _log_sigmoid(z):
    return jnp.minimum(z, 0.0) - jnp.log1p(jnp.exp(-jnp.abs(z)))


def _params(n_axes):
    return pltpu.CompilerParams(dimension_semantics=("arbitrary",) * n_axes,
                                vmem_limit_bytes=VMEM_LIMIT)


def _resident(shape):
    nd = len(shape)
    return pl.BlockSpec(shape, lambda *_: (0,) * nd, pipeline_mode=pl.Buffered(1))


def _rope_rows(ref_head, rows, cos, sin):
    x1 = rows[0:ROPE_HALF]
    x2 = rows[ROPE_HALF:ROPE_DIM]
    ref_head[0:ROPE_HALF] = x1 * cos - x2 * sin
    ref_head[ROPE_HALF:ROPE_DIM] = x2 * cos + x1 * sin
    ref_head[ROPE_DIM:] = rows[ROPE_DIM:]


def _inproj_kernel(x_ref, wt_ref, wk_ref, wf_ref, wfr_ref, wg_ref, bf_ref, bfr_ref, cq_ref, sq_ref, cos_ref, sin_ref,
                   qt_ref, krow_ref, kvsb_ref, kvmb_ref, kvfx_ref, logf_ref, cbc_ref, gate_ref, carry_ref, *, tm):
    t = pl.program_id(1)
    x = x_ref[0]
    xb, xlo = _split2(x)
    cos = cos_ref[...]
    sin = sin_ref[...]

    pt = _dot_nt(wt_ref[...], xb)
    head = lambda r: pt[r:r + HEAD_DIM]
    for h in range(H_SB):
        qt_ref[0, h] = head(h * HEAD_DIM)
    for h in range(H_MB):
        _rope_rows(qt_ref.at[0, H_SB + h], head(W_SB + h * HEAD_DIM), cos, sin)
    for h in range(H_FX):
        qt_ref[0, H_SB + H_MB + h] = head(W_SB + W_MB + h * HEAD_DIM)
    base = W_ALL
    for h in range(2 * H_SB):
        kvsb_ref[0, h // H_SB, h % H_SB] = head(base + h * HEAD_DIM)
    base += 2 * W_SB
    for h in range(H_MB):
        _rope_rows(kvmb_ref.at[0, 0, h], head(base + h * HEAD_DIM), cos, sin)
        kvmb_ref[0, 1, h] = head(base + W_MB + h * HEAD_DIM)
    base += 2 * W_MB
    for h in range(2 * H_FX):
        kvfx_ref[0, h // H_FX, h % H_FX] = head(base + h * HEAD_DIM)

    kr = _dot(xb, wk_ref[...])
    k_mb = kr[:, W_SB:W_SB + W_MB] * cq_ref[...] + kr[:, W_ALL:] * sq_ref[...]
    for h in range(H_SB):
        krow_ref[0, h] = kr[:, h * HEAD_DIM:(h + 1) * HEAD_DIM].astype(bf16)
    for h in range(H_MB):
        krow_ref[0, H_SB + h] = k_mb[:, h * HEAD_DIM:(h + 1) * HEAD_DIM].astype(bf16)
    for h in range(H_FX):
        o = W_SB + W_MB + h * HEAD_DIM
        krow_ref[0, H_SB + H_MB + h] = kr[:, o:o + HEAD_DIM].astype(bf16)

    r1 = _dot_nt(wf_ref[...], xb)
    r2 = _dot_nt(wf_ref[0:SUBLANES], xlo)
    logf_ref[0] = _log_sigmoid(r1[0:SUBLANES] + r1[SUBLANES:] + r2 + bf_ref[...])

    f1 = _dot(xb, wfr_ref[...])
    f2 = _dot(xlo, wfr_ref[:, 0:LANES])
    logf_rows = _log_sigmoid(f1[:, 0:LANES] + f1[:, LANES:] + f2 + bfr_ref[...])

    @pl.when(t == 0)
    def _():
        carry_ref[...] = jnp.zeros_like(carry_ref)
    row = lax.broadcasted_iota(jnp.int32, (tm, tm), 0)
    col = lax.broadcasted_iota(jnp.int32, (tm, tm), 1)
    c_rows = _dot3_left((col <= row).astype(bf16), logf_rows) + carry_ref[...]
    carry_ref[...] = c_rows[tm - 1:tm]
    src = lax.broadcasted_iota(jnp.int32, (LANES, LANES), 0)
    for h in range(H_FX):
        cbc_ref[0, h] = _dot3(c_rows, (src == h).astype(bf16))

    g = _dot(xb, wg_ref[...])
    gate_ref[0] = (1.0 / (1.0 + jnp.exp(-g))).astype(gate_ref.dtype)


def _inproj(x, w, rope, tm):
    B, T, D = x.shape
    wt, wk, wf, wfr, wg, bfc, bfr = w
    cq, sq, cos_t, sin_t = rope
    tok = lambda b, t: (b, t, 0)
    out_shape = (
        jax.ShapeDtypeStruct((B, H_ALL, HEAD_DIM, T), f32),
        jax.ShapeDtypeStruct((B, H_ALL, T, HEAD_DIM), bf16),
        jax.ShapeDtypeStruct((B, 2, H_SB, HEAD_DIM, T), f32),
        jax.ShapeDtypeStruct((B, 2, H_MB, HEAD_DIM, T), f32),
        jax.ShapeDtypeStruct((B, 2, H_FX, HEAD_DIM, T), f32),
        jax.ShapeDtypeStruct((B, SUBLANES, T), f32),
        jax.ShapeDtypeStruct((B, H_FX, T, LANES), f32),
        jax.ShapeDtypeStruct((B, T, N_BRANCH * D), bf16),
    )
    kv_spec = lambda H: pl.BlockSpec((1, 2, H, HEAD_DIM, tm), lambda b, t: (b, 0, 0, 0, t))
    return pl.pallas_call(
        functools.partial(_inproj_kernel, tm=tm),
        grid=(B, T // tm),
        in_specs=[pl.BlockSpec((1, tm, D), tok)] + [_resident(a.shape) for a in w]
        + [pl.BlockSpec((tm, W_MB), lambda b, t: (t, 0)),
           pl.BlockSpec((tm, W_MB), lambda b, t: (t, 0)),
           pl.BlockSpec((ROPE_HALF, tm), lambda b, t: (0, t)),
           pl.BlockSpec((ROPE_HALF, tm), lambda b, t: (0, t))],
        out_specs=(pl.BlockSpec((1, H_ALL, HEAD_DIM, tm), lambda b, t: (b, 0, 0, t)),
                   pl.BlockSpec((1, H_ALL, tm, HEAD_DIM), lambda b, t: (b, 0, t, 0)),
                   kv_spec(H_SB), kv_spec(H_MB), kv_spec(H_FX),
                   pl.BlockSpec((1, SUBLANES, tm), lambda b, t: (b, 0, t)),
                   pl.BlockSpec((1, H_FX, tm, LANES), lambda b, t: (b, 0, t, 0)),
                   pl.BlockSpec((1, tm, N_BRANCH * D), tok)),
        out_shape=out_shape,
        scratch_shapes=[pltpu.VMEM((1, LANES), f32)],
        compiler_params=_params(2),
        name="inproj",
    )(x, *w, cq, sq, cos_t, sin_t)


def _tile_iotas(n):
    row = lax.broadcasted_iota(jnp.int32, (n, n), 0)
    col = lax.broadcasted_iota(jnp.int32, (n, n), 1)
    return row, col


def _sb_kernel(qt_ref, kr_ref, v_ref, o_ref, carry_ref, acc_ref, *, tq):
    i = pl.program_id(2)
    row, col = _tile_iotas(tq)
    later = (col > row).astype(bf16)
    strict = row < col

    def block(j, diag):
        start = pl.multiple_of(j * tq, tq)
        carry = [carry_ref[g] for g in range(SB_HEAD_GROUP)]
        acc = [acc_ref[g] for g in range(SB_HEAD_GROUP)]
        for g in range(SB_HEAD_GROUP):
            qtb = qt_ref[0, g].astype(bf16)
            zt = _dot(kr_ref[0, g, pl.ds(start, tq), :], qtb)
            l1m = jnp.minimum(-zt, 0.0) - jnp.log(1.0 + jnp.exp(jnp.minimum(zt, -zt)))
            if diag:
                l1m = jnp.where(strict, l1m, 0.0)
            hi, lo = _split2(l1m)
            after = _dot(later, hi) + _dot(later, lo) + carry[g]
            w = jnp.exp(zt + l1m + after)
            if diag:
                w = jnp.where(strict, w, 0.0)
            vt = v_ref[0, 0, g, :, pl.ds(start, tq)].astype(bf16)
            acc[g] = acc[g] + _dot(vt, w.astype(bf16))
            carry[g] = carry[g] + jnp.sum(l1m, axis=0, keepdims=True)
        for g in range(SB_HEAD_GROUP):
            acc_ref[g] = acc[g]
            carry_ref[g] = carry[g]

    acc_ref[...] = jnp.zeros_like(acc_ref)
    carry_ref[...] = jnp.zeros_like(carry_ref)
    block(i, True)

    def cond(s):
        j, top = s
        return jnp.logical_and(j >= 0, top > EXP_ZERO_BELOW)

    def body(s):
        j, _ = s
        block(j, False)
        return j - 1, jnp.max(carry_ref[...])

    lax.while_loop(cond, body, (i - 1, jnp.max(carry_ref[...])))
    for g in range(SB_HEAD_GROUP):
        o_ref[0, g] = jnp.transpose(acc_ref[g]).astype(o_ref.dtype)


def _softmax_init(m_ref, l_ref, acc_ref):
    m_ref[...] = jnp.full_like(m_ref, NEG)
    l_ref[...] = jnp.zeros_like(l_ref)
    acc_ref[...] = jnp.zeros_like(acc_ref)


def _softmax_step(st, vt, m_ref, l_ref, acc_ref):
    m_old = m_ref[...]
    m_new = jnp.maximum(m_old, jnp.max(st, axis=0, keepdims=True))
    alpha = jnp.exp2(m_old - m_new)
    p = jnp.exp2(st - m_new)
    l_ref[...] = alpha * l_ref[...] + jnp.sum(p, axis=0, keepdims=True)
    acc_ref[...] = alpha * acc_ref[...] + _dot(vt, p.astype(bf16))
    m_ref[...] = m_new


def _softmax_finish(o_ref, l_ref, acc_ref):
    o_ref[0, 0] = jnp.transpose(acc_ref[...] / l_ref[...]).astype(o_ref.dtype)


def _causal(chunk_start, query_start, tk, tq):
    row = lax.broadcasted_iota(jnp.int32, (tk, tq), 0)
    col = lax.broadcasted_iota(jnp.int32, (tk, tq), 1)
    return row - col <= query_start - chunk_start


def _fox_kernel(qt_ref, kr_ref, v_ref, cbc_ref, o_ref, kn_ref, m_ref, l_ref, acc_ref, *, tq, kb):
    i = pl.program_id(2)
    tk = kb * tq
    n_keys = kr_ref.shape[2]

    @pl.when(i == 0)
    def _():
        kn_ref[...] = _max_key_norm(kr_ref)

    qt = qt_ref[0, 0]
    qtb = (qt * LOG2E).astype(bf16)
    q_norm = jnp.sqrt(jnp.max(jnp.sum(qt * qt, axis=0, keepdims=True), axis=1, keepdims=True))
    qk_bound = 1.01 * q_norm * kn_ref[...] + 0.01

    c_base = cbc_ref[0, 0, pl.ds(pl.multiple_of(i * tq, tq), 1), :]

    def scores(c, diag):
        start = pl.multiple_of(c * tk, tk)
        st = _dot(kr_ref[0, 0, pl.ds(start, tk), :], qtb)
        cb = (cbc_ref[0, 0, pl.ds(start, tk), :] - c_base) * LOG2E
        st = st - jnp.concatenate([cb] * (tq // LANES), axis=1)
        if diag:
            st = jnp.where(_causal(start, i * tq, tk, tq), st, NEG)
        return st

    def absorb(st, c):
        start = pl.multiple_of(c * tk, tk)
        vt = v_ref[0, 0, 0, :, pl.ds(start, tk)].astype(bf16)
        _softmax_step(st, vt, m_ref, l_ref, acc_ref)

    def reaches(c):
        last = jnp.maximum((c + 1) * tk - 1, 0)
        c_last = cbc_ref[0, 0, pl.ds(last, 1), :] - c_base
        m_low = jnp.min(m_ref[...], axis=1, keepdims=True)
        return jnp.max((qk_bound - c_last) * LOG2E - m_low) >= EXP_ZERO_BELOW * LOG2E

    _softmax_init(m_ref, l_ref, acc_ref)
    c0 = i // kb

    absorb(scores(c0, True), c0)

    def cond(s):
        c, go = s
        return jnp.logical_and(c >= 0, go)

    def body(s):
        c, _ = s
        absorb(scores(c, False), c)
        return c - 1, reaches(c - 1)

    lax.while_loop(cond, body, (c0 - 1, reaches(c0 - 1)))
    _softmax_finish(o_ref, l_ref, acc_ref)


def _top_blocks(gate, cand, idx, axis):
    n = gate.shape[axis]
    gate = jnp.where(cand, gate, NEG)
    chosen = jnp.zeros(gate.shape, f32)
    firsts = []
    for _ in range(MOBA_TOPK):
        top = jnp.max(gate, axis=axis, keepdims=True)
        first = jnp.min(jnp.where(gate == top, idx, n), axis=axis, keepdims=True)
        pick = idx == first
        chosen = jnp.where(jnp.logical_and(pick, cand), 1.0, chosen)
        gate = jnp.where(pick, -jnp.inf, gate)
        firsts.append(first)
    return chosen, firsts


def _max_key_norm(kr_ref):
    def body(n, top):
        k = kr_ref[0, 0, pl.ds(pl.multiple_of(n * NORM_ROWS, NORM_ROWS), NORM_ROWS), :].astype(f32)
        return jnp.maximum(top, jnp.max(jnp.sum(k * k, axis=1, keepdims=True), axis=0, keepdims=True))
    return jnp.sqrt(lax.fori_loop(0, kr_ref.shape[2] // NORM_ROWS, body, jnp.zeros((1, 1), f32)))


def _moba_kernel(qt_ref, kr_ref, k_ref, v_ref, o_ref, km_ref, ch_ref, kn_ref, m_ref, l_ref, acc_ref, *, tq, kb, nb):
    i = pl.program_id(2)
    tk = kb * tq

    @pl.when(i == 0)
    def _():
        kn_ref[...] = _max_key_norm(kr_ref)
        lane = lax.broadcasted_iota(jnp.int32, (HEAD_DIM, LANES), 1)

        def body(j, km):
            start = pl.multiple_of(j * tq, tq)
            blk = k_ref[0, 0, 0, :, pl.ds(start, tq)]
            mean = jnp.sum(blk, axis=1, keepdims=True) * (1.0 / tq)
            return jnp.where(lane == j, mean, km)

        km_ref[...] = jnp.transpose(lax.fori_loop(0, nb, body, jnp.zeros((HEAD_DIM, LANES), f32)))

    qt = qt_ref[0, 0]
    qtb = (qt * LOG2E).astype(bf16)
    qh, ql = _split2(qt)
    kh, kl = _split2(km_ref[...])
    gate = _dot(kh, qh) + _dot(kh, ql) + _dot(kl, qh)
    blk = lax.broadcasted_iota(jnp.int32, (LANES, tq), 0)
    chosen, _ = _top_blocks(gate, blk < i, blk, 0)
    ch_ref[...] = jnp.where(blk == i, 1.0, chosen)

    def scores(c, diag):
        start = pl.multiple_of(c * tk, tk)
        st = _dot(kr_ref[0, 0, pl.ds(start, tk), :], qtb)
        picked = jnp.concatenate(
            [jnp.broadcast_to(ch_ref[pl.ds(c * kb + t, 1), :], (tq, tq)) for t in range(kb)], axis=0) > 0.0
        if diag:
            picked = jnp.logical_and(picked, _causal(start, i * tq, tk, tq))
        return jnp.where(picked, st, NEG)

    def values(c):
        start = pl.multiple_of(c * tk, tk)
        return v_ref[0, 0, 0, :, pl.ds(start, tk)].astype(bf16)

    def absorb_online(st, c):
        _softmax_step(st, values(c), m_ref, l_ref, acc_ref)

    q_norm = jnp.sqrt(jnp.sum(qt * qt, axis=0, keepdims=True))
    stab = (1.01 * q_norm * kn_ref[...] + 0.01) * LOG2E

    def absorb_fixed(st, c):
        p = jnp.exp2(st - stab)
        l_ref[...] += jnp.sum(p, axis=0, keepdims=True)
        acc_ref[...] += _dot(values(c), p.astype(bf16))

    def run(absorb):
        absorb(scores(c0, True), c0)

        def body(c, _):
            absorb(scores(c, False), c)
            return 0

        lax.fori_loop(0, c0, body, 0)

    _softmax_init(m_ref, l_ref, acc_ref)
    c0 = i // kb
    fixed_ok = jnp.max(stab) < FIXED_STABILISER_LIMIT

    @pl.when(fixed_ok)
    def _():
        run(absorb_fixed)

    @pl.when(jnp.logical_not(fixed_ok))
    def _():
        run(absorb_online)

    _softmax_finish(o_ref, l_ref, acc_ref)


def _softmax_scratch(tq):
    return [pltpu.VMEM((1, tq), f32), pltpu.VMEM((1, tq), f32), pltpu.VMEM((HEAD_DIM, tq), f32)]


def _prompt_attention(kernel, name, head_off, n_heads, operands, specs, scratch):
    qt = operands[0]
    B, _, hd, T = qt.shape
    return pl.pallas_call(
        kernel,
        grid=(B, n_heads, T // TQ),
        in_specs=[pl.BlockSpec((1, 1, hd, TQ), lambda b, h, i: (b, h + head_off, 0, i))] + specs,
        out_specs=pl.BlockSpec((1, 1, TQ, hd), lambda b, h, i: (b, h, i, 0)),
        out_shape=jax.ShapeDtypeStruct((B, n_heads, T, hd), bf16),
        scratch_shapes=scratch,
        compiler_params=_params(3),
        name=name,
    )(*operands)


def _kvt_spec(T, kv):
    return pl.BlockSpec((1, 1, 1, HEAD_DIM, T), lambda b, h, i: (b, kv, h, 0, 0))


def _krow_spec(T, head_off):
    return pl.BlockSpec((1, 1, T, HEAD_DIM), lambda b, h, i: (b, h + head_off, 0, 0))


def _sb_prompt(qt, krow, kvt):
    B, _, hd, T = qt.shape
    G = SB_HEAD_GROUP
    assert H_SB % G == 0
    return pl.pallas_call(
        functools.partial(_sb_kernel, tq=TQ),
        grid=(B, H_SB // G, T // TQ),
        in_specs=[pl.BlockSpec((1, G, hd, TQ), lambda b, h, i: (b, h, 0, i)),
                  pl.BlockSpec((1, G, T, hd), lambda b, h, i: (b, h, 0, 0)),
                  pl.BlockSpec((1, 1, G, hd, T), lambda b, h, i: (b, 1, h, 0, 0))],
        out_specs=pl.BlockSpec((1, G, TQ, hd), lambda b, h, i: (b, h, i, 0)),
        out_shape=jax.ShapeDtypeStruct((B, H_SB, T, hd), bf16),
        scratch_shapes=[pltpu.VMEM((G, 1, TQ), f32), pltpu.VMEM((G, hd, TQ), f32)],
        compiler_params=_params(3),
        name="sb_prompt",
    )(qt, krow, kvt)


def _moba_prompt(qt, krow, kvt):
    T = qt.shape[3]
    assert T % (MOBA_CHUNK_BLOCKS * TQ) == 0 and T // TQ <= LANES and T % NORM_ROWS == 0
    return _prompt_attention(functools.partial(_moba_kernel, tq=TQ, kb=MOBA_CHUNK_BLOCKS, nb=T // TQ),
                             "moba_prompt", H_SB, H_MB, (qt, krow, kvt, kvt),
                             [_krow_spec(T, H_SB), _kvt_spec(T, 0), _kvt_spec(T, 1)],
                             [pltpu.VMEM((LANES, HEAD_DIM), f32), pltpu.VMEM((LANES, TQ), f32),
                              pltpu.VMEM((1, 1), f32)] + _softmax_scratch(TQ))


def _fox_prompt(qt, krow, kvt, cbc):
    T = qt.shape[3]
    assert T % (FOX_CHUNK_BLOCKS * TQ) == 0 and T % NORM_ROWS == 0
    return _prompt_attention(functools.partial(_fox_kernel, tq=TQ, kb=FOX_CHUNK_BLOCKS), "fox_prompt",
                             H_SB + H_MB, H_FX, (qt, krow, kvt, cbc),
                             [_krow_spec(T, H_SB + H_MB), _kvt_spec(T, 1),
                              pl.BlockSpec((1, 1, T, LANES), lambda b, h, i: (b, h, 0, 0))],
                             [pltpu.VMEM((1, 1), f32)] + _softmax_scratch(TQ))


def _layer_norm(x, g, b):
    mu = jnp.mean(x, axis=-1, keepdims=True)
    xc = x - mu
    var = jnp.mean(xc * xc, axis=-1, keepdims=True)
    return xc * lax.rsqrt(var + LN_EPS) * g + b


def _post_kernel(x_ref, osb_ref, omb_ref, ofx_ref, gate_ref, wsb_ref, wmb_ref, wfx_ref, wo_ref, wup_ref, wdn_ref,
                 g1_ref, b1_ref, g2_ref, b2_ref, y_ref):
    x = x_ref[0]
    merged = None
    for br, (o_ref, w_ref) in enumerate(((osb_ref, wsb_ref), (omb_ref, wmb_ref), (ofx_ref, wfx_ref))):
        pb = None
        for h in range(o_ref.shape[1]):
            t = _dot(o_ref[0, h], w_ref[h])
            pb = t if pb is None else pb + t
        gb = gate_ref[0, :, br * D_MODEL:(br + 1) * D_MODEL].astype(f32)
        merged = gb * pb if merged is None else merged + gb * pb
    u = _dot(merged.astype(bf16), wo_ref[...])
    x1 = _layer_norm(DN_ALPHA * x + u, g1_ref[...], b1_ref[...])
    x1b = x1.astype(bf16)
    acc = None
    for f in range(D_FF // FF_CHUNK):
        hm = jnp.maximum(_dot(x1b, wup_ref[:, f * FF_CHUNK:(f + 1) * FF_CHUNK]), 0.0)
        t = _dot((hm * hm).astype(bf16), wdn_ref[f * FF_CHUNK:(f + 1) * FF_CHUNK, :])
        acc = t if acc is None else acc + t
    y_ref[0] = _layer_norm(DN_ALPHA * x1 + acc, g2_ref[...], b2_ref[...])


def _post(x, o_sb, o_mb, o_fx, gates, w, tm):
    B, T, D = x.shape
    tok = lambda b, t: (b, t, 0)
    o_spec = lambda H: pl.BlockSpec((1, H, tm, HEAD_DIM), lambda b, t: (b, 0, t, 0))
    return pl.pallas_call(
        _post_kernel,
        grid=(B, T // tm),
        in_specs=[pl.BlockSpec((1, tm, D), tok), o_spec(H_SB), o_spec(H_MB), o_spec(H_FX),
                  pl.BlockSpec((1, tm, N_BRANCH * D), tok)] + [_resident(a.shape) for a in w],
        out_specs=pl.BlockSpec((1, tm, D), tok),
        out_shape=jax.ShapeDtypeStruct((B, T, D), f32),
        compiler_params=_params(2),
        name="post",
    )(x, o_sb, o_mb, o_fx, gates, *w)


def _page_scores(z_ref, k_page, qb_ref, n_heads):
    for h in range(n_heads):
        z_ref[h:h + 1, :] = jnp.sum(k_page[h] * qb_ref[0, h], axis=0, keepdims=True)


def _later_matrix():
    row, col = _tile_iotas(PAGE)
    return (row > col).astype(bf16)


def _finish_heads(o_ref, acc_ref, n_heads, scale=None):
    o_ref[0] = jnp.zeros(o_ref.shape[1:], f32)
    for h in range(n_heads):
        r = _lane_sum_rows(acc_ref[h])[0:1]
        if scale is not None:
            r = r * scale[h:h + 1, 0:HEAD_DIM]
        o_ref[0, h:h + 1, :] = r


def _sb_pages(pages, qb_ref, z_ref, acc_ref, carry_ref, n_heads):
    later = _later_matrix()
    for page in pages:
        @pl.when(jnp.max(carry_ref[0:n_heads]) > EXP_ZERO_BELOW)
        def _():
            _page_scores(z_ref, page.at[0], qb_ref, n_heads)
            z = z_ref[...]
            l1m = -_softplus(z)
            hi, lo = _split2(l1m)
            after = _dot(hi, later) + _dot(lo, later) + carry_ref[...]
            w = jnp.exp(z + l1m + after)
            for h in range(n_heads):
                acc_ref[h] += page[1, h] * w[h:h + 1, :]
            carry_ref[...] += jnp.sum(l1m, axis=1, keepdims=True)


def _sb_dec_head_kernel(pt_ref, qb_ref, *refs, n_heads):
    pages = refs[:PAGES_PER_STEP]
    acc_out, carry_out, z_ref, carry_ref = refs[PAGES_PER_STEP:]
    z_ref[...] = jnp.zeros_like(z_ref)
    carry_ref[...] = jnp.zeros_like(carry_ref)
    acc_out[...] = jnp.zeros_like(acc_out)
    _sb_pages(pages, qb_ref, z_ref, acc_out.at[0], carry_ref, n_heads)
    carry_out[0] = jnp.broadcast_to(carry_ref[...], (SUBLANES, LANES))


def _sb_dec_tail_kernel(pt_ref, done_ref, qb_ref, acc_in, carry_in, *refs, n_heads):
    pages = refs[:PAGES_PER_STEP]
    o_ref, z_ref, acc_ref, carry_ref = refs[PAGES_PER_STEP:]
    b = pl.program_id(0)
    c = pl.program_id(1)

    @pl.when(c == 0)
    def _():
        z_ref[...] = jnp.zeros_like(z_ref)
        acc_ref[...] = acc_in[0]
        carry_ref[...] = carry_in[0][:, 0:1]

    @pl.when(done_ref[b] == 0)
    def _():
        _sb_pages(pages, qb_ref, z_ref, acc_ref, carry_ref, n_heads)

    @pl.when(c == pl.num_programs(1) - 1)
    def _():
        _finish_heads(o_ref, acc_ref, n_heads)


def _fox_dec_kernel(pt_ref, qb_ref, kb_ref, vb_ref, lfn_ref, *refs, n_heads):
    pages = refs[:PAGES_PER_STEP]
    lf_pages = refs[PAGES_PER_STEP:2 * PAGES_PER_STEP]
    o_ref, z_ref, lf_ref, m_ref, l_ref, acc_ref, carry_ref = refs[2 * PAGES_PER_STEP:]
    c = pl.program_id(1)

    @pl.when(c == 0)
    def _():
        z_ref[...] = jnp.zeros_like(z_ref)
        lf_ref[...] = jnp.zeros_like(lf_ref)
        carry_ref[...] = jnp.zeros_like(carry_ref)
        _page_scores(z_ref, kb_ref.at[0], qb_ref, n_heads)
        m_ref[...] = z_ref[...]
        l_ref[...] = jnp.ones_like(l_ref)
        lane0 = lax.broadcasted_iota(jnp.int32, (HEAD_DIM, LANES), 1) == 0
        for h in range(n_heads):
            acc_ref[h] = jnp.where(lane0, vb_ref[0, h], 0.0)

    later = _later_matrix()
    for page, lf_page in zip(pages, lf_pages):
        _page_scores(z_ref, page.at[0], qb_ref, n_heads)
        for h in range(n_heads):
            lf_ref[h:h + 1, :] = lf_page[h]
        lf = lf_ref[...]
        s = z_ref[...] + _dot3(lf, later) + carry_ref[...] + lfn_ref[0]
        m_old = m_ref[...]
        m_new = jnp.maximum(m_old, jnp.max(s, axis=1, keepdims=True))
        alpha = jnp.exp(m_old - m_new)
        p = jnp.exp(s - m_new)
        l_ref[...] = alpha * l_ref[...] + jnp.sum(p, axis=1, keepdims=True)
        m_ref[...] = m_new
        for h in range(n_heads):
            acc_ref[h] = acc_ref[h] * alpha[h:h + 1, :] + page[1, h] * p[h:h + 1, :]
        carry_ref[...] += jnp.sum(lf, axis=1, keepdims=True)

    @pl.when(c == pl.num_programs(1) - 1)
    def _():
        _finish_heads(o_ref, acc_ref, n_heads, scale=1.0 / l_ref[...])


def _cache_view(cache):
    return jnp.transpose(cache, (0, 1, 3, 4, 5, 2))


def _bcast_spec(n_heads):
    return pl.BlockSpec((1, n_heads, HEAD_DIM, LANES), lambda b, c, pt: (b, 0, 0, 0))


def _page_specs(layer, n_heads, n_pages, newest_first, kv=None):
    specs = []
    for i in range(PAGES_PER_STEP):
        def page_of(b, c, pt, i=i):
            lp = c * PAGES_PER_STEP + i
            return pt[b, n_pages - 1 - lp] if newest_first else pt[b, lp]
        if kv is None:
            specs.append(pl.BlockSpec((None, None, 2, n_heads, HEAD_DIM, PAGE),
                                      lambda b, c, pt, f=page_of: (layer, f(b, c, pt), 0, 0, 0, 0)))
        else:
            specs.append(pl.BlockSpec((None, None, None, n_heads, HEAD_DIM, PAGE),
                                      lambda b, c, pt, f=page_of: (layer, f(b, c, pt), kv, 0, 0, 0)))
    return specs


def _dec_out(nseq):
    return (pl.BlockSpec((1, SUBLANES, HEAD_DIM), lambda b, c, pt: (b, 0, 0)),
            jax.ShapeDtypeStruct((nseq, SUBLANES, HEAD_DIM), f32))


def _sb_decode(cache_t, layer, page_table, qb):
    nseq, n_pages = page_table.shape
    n_heads = cache_t.shape[3]
    steps = n_pages // PAGES_PER_STEP
    assert steps >= 2
    state = pl.BlockSpec((1, n_heads, HEAD_DIM, LANES), lambda b, c, *_: (b, 0, 0, 0))
    row8 = pl.BlockSpec((1, SUBLANES, LANES), lambda b, c, *_: (b, 0, 0))
    z_scratch = pltpu.VMEM((SUBLANES, PAGE), f32)
    carry_scratch = pltpu.VMEM((SUBLANES, 1), f32)
    gs = pltpu.PrefetchScalarGridSpec(
        num_scalar_prefetch=1, grid=(nseq, 1),
        in_specs=[_bcast_spec(n_heads)] + _page_specs(layer, n_heads, n_pages, True),
        out_specs=(state, row8), scratch_shapes=[z_scratch, carry_scratch])
    acc, carry = pl.pallas_call(
        functools.partial(_sb_dec_head_kernel, n_heads=n_heads), grid_spec=gs,
        out_shape=(jax.ShapeDtypeStruct((nseq, n_heads, HEAD_DIM, LANES), f32),
                   jax.ShapeDtypeStruct((nseq, SUBLANES, LANES), f32)),
        compiler_params=_params(2), name="sb_decode_head",
    )(page_table, qb, *([cache_t] * PAGES_PER_STEP))

    done = (jnp.max(carry[:, :n_heads, 0], axis=1) <= EXP_ZERO_BELOW).astype(jnp.int32)
    specs = []
    for i in range(PAGES_PER_STEP):
        def page_of(b, c, pt, done, i=i):
            lp = (c + 1) * PAGES_PER_STEP + i
            return jnp.where(done[b] == 0, pt[b, n_pages - 1 - lp], 0)
        specs.append(pl.BlockSpec((None, None, 2, n_heads, HEAD_DIM, PAGE),
                                  lambda b, c, pt, done, f=page_of: (layer, f(b, c, pt, done), 0, 0, 0, 0)))
    out_spec, out_shape = _dec_out(nseq)
    gs = pltpu.PrefetchScalarGridSpec(
        num_scalar_prefetch=2, grid=(nseq, steps - 1),
        in_specs=[pl.BlockSpec((1, n_heads, HEAD_DIM, LANES), lambda b, c, *_: (b, 0, 0, 0)), state, row8] + specs,
        out_specs=pl.BlockSpec((1, SUBLANES, HEAD_DIM), lambda b, c, *_: (b, 0, 0)),
        scratch_shapes=[z_scratch, pltpu.VMEM((n_heads, HEAD_DIM, PAGE), f32), carry_scratch])
    return pl.pallas_call(
        functools.partial(_sb_dec_tail_kernel, n_heads=n_heads), grid_spec=gs, out_shape=out_shape,
        compiler_params=_params(2), name="sb_decode_tail",
    )(page_table, done, qb, acc, carry, *([cache_t] * PAGES_PER_STEP))


def _fox_decode(cache_t, logf_t, layer, page_table, qb, kb, vb, lfn):
    nseq, n_pages = page_table.shape
    n_heads = cache_t.shape[3]
    out_spec, out_shape = _dec_out(nseq)
    lf_specs = []
    for i in range(PAGES_PER_STEP):
        lf_specs.append(pl.BlockSpec(
            (None, n_heads, None, 1, PAGE),
            lambda b, c, pt, i=i: (layer, 0, pt[b, n_pages - 1 - (c * PAGES_PER_STEP + i)], 0, 0)))
    gs = pltpu.PrefetchScalarGridSpec(
        num_scalar_prefetch=1, grid=(nseq, n_pages // PAGES_PER_STEP),
        in_specs=[_bcast_spec(n_heads)] * 3 + [pl.BlockSpec((1, SUBLANES, LANES), lambda b, c, pt: (b, 0, 0))]
        + _page_specs(layer, n_heads, n_pages, True) + lf_specs,
        out_specs=out_spec,
        scratch_shapes=[pltpu.VMEM((SUBLANES, PAGE), f32), pltpu.VMEM((SUBLANES, PAGE), f32),
                        pltpu.VMEM((SUBLANES, PAGE), f32), pltpu.VMEM((SUBLANES, PAGE), f32),
                        pltpu.VMEM((n_heads, HEAD_DIM, PAGE), f32), pltpu.VMEM((SUBLANES, 1), f32)])
    return pl.pallas_call(functools.partial(_fox_dec_kernel, n_heads=n_heads), grid_spec=gs, out_shape=out_shape,
                          compiler_params=_params(2), name="fox_decode",
                          )(page_table, qb, kb, vb, lfn, *([cache_t] * PAGES_PER_STEP),
                            *([logf_t] * PAGES_PER_STEP))


def _mb_scores_kernel(pt_ref, qb_ref, kb_ref, *refs, n_heads):
    pages = refs[:PAGES_PER_STEP]
    z_out, bsum_ref, zself_ref, z_ref = refs[PAGES_PER_STEP:]
    c = pl.program_id(1)

    @pl.when(c == 0)
    def _():
        z_ref[...] = jnp.zeros_like(z_ref)
        bsum_ref[...] = jnp.zeros_like(bsum_ref)
        _page_scores(z_ref, kb_ref.at[0], qb_ref, n_heads)
        zself_ref[0] = z_ref[...]

    lane = lax.broadcasted_iota(jnp.int32, (SUBLANES, LANES), 1)
    for i, page in enumerate(pages):
        _page_scores(z_ref, page, qb_ref, n_heads)
        z = z_ref[...]
        z_out[0, :, i * PAGE:(i + 1) * PAGE] = z
        blk = (c * PAGES_PER_STEP + i) // PAGES_PER_BLOCK
        bsum_ref[0] += jnp.where(lane == blk, jnp.sum(z, axis=1, keepdims=True), 0.0)


def _mb_select_kernel(z_ref, bsum_ref, zself_ref, p_ref, pself_ref, idx_ref, *, n_blocks):
    gate = bsum_ref[0] * (1.0 / MOBA_BLOCK)
    lane = lax.broadcasted_iota(jnp.int32, (SUBLANES, LANES), 1)
    _, firsts = _top_blocks(gate, lane < n_blocks, lane, 1)
    idx = jnp.zeros((SUBLANES, LANES), jnp.int32)
    for r, first in enumerate(firsts):
        idx = jnp.where(lane == r, first, idx)
    idx_ref[0] = idx
    n_keys = z_ref.shape[2]
    blk_of_key = lax.broadcasted_iota(jnp.int32, (SUBLANES, n_keys), 1) // MOBA_BLOCK
    picked = None
    for first in firsts:
        hit = jnp.logical_and(blk_of_key == first, first < n_blocks)
        picked = hit if picked is None else jnp.logical_or(picked, hit)
    s = jnp.where(picked, z_ref[0], NEG)
    zs = zself_ref[0]
    m = jnp.maximum(jnp.max(s, axis=1, keepdims=True), zs[:, 0:1])
    p = jnp.exp(s - m)
    ps = jnp.exp(zs - m)
    inv = 1.0 / (jnp.sum(p, axis=1, keepdims=True) + ps[:, 0:1])
    p_ref[0] = p * inv
    pself_ref[0] = ps * inv


N_PICKED_PAGES = MOBA_TOPK * PAGES_PER_BLOCK


def _mb_pv_kernel(pt_ref, idx_ref, *refs):
    v_pages = refs[:N_PICKED_PAGES]
    p_pages = refs[N_PICKED_PAGES:2 * N_PICKED_PAGES]
    pself_ref, vb_ref, o_ref = refs[2 * N_PICKED_PAGES:]
    h = pl.program_id(1)

    @pl.when(h == 0)
    def _():
        o_ref[...] = jnp.zeros_like(o_ref)

    lane0 = lax.broadcasted_iota(jnp.int32, (HEAD_DIM, LANES), 1) == 0
    acc = jnp.where(lane0, vb_ref[0, h] * pself_ref[0, pl.ds(h, 1), :], 0.0)
    for v_page, p_page in zip(v_pages, p_pages):
        acc = acc + v_page[...] * p_page[0, pl.ds(h, 1), :]
    o_ref[0, pl.ds(h, 1), :] = _lane_sum_rows(acc)[0:1]


def _moba_decode(cache_t, layer, page_table, qb, kb, vb):
    nseq, n_pages = page_table.shape
    n_heads = cache_t.shape[3]
    n_keys = n_pages * PAGE
    n_blocks = n_keys // MOBA_BLOCK
    assert MOBA_TOPK <= n_blocks <= LANES
    row8 = pl.BlockSpec((1, SUBLANES, LANES), lambda b, c, pt: (b, 0, 0))
    gs = pltpu.PrefetchScalarGridSpec(
        num_scalar_prefetch=1, grid=(nseq, n_pages // PAGES_PER_STEP),
        in_specs=[_bcast_spec(n_heads)] * 2 + _page_specs(layer, n_heads, n_pages, False, kv=0),
        out_specs=(pl.BlockSpec((1, SUBLANES, PAGES_PER_STEP * PAGE), lambda b, c, pt: (b, 0, c)), row8, row8),
        scratch_shapes=[pltpu.VMEM((SUBLANES, PAGE), f32)])
    z, bsum, zself = pl.pallas_call(
        functools.partial(_mb_scores_kernel, n_heads=n_heads), grid_spec=gs,
        out_shape=(jax.ShapeDtypeStruct((nseq, SUBLANES, n_keys), f32),
                   jax.ShapeDtypeStruct((nseq, SUBLANES, LANES), f32),
                   jax.ShapeDtypeStruct((nseq, SUBLANES, LANES), f32)),
        compiler_params=_params(2), name="moba_decode_scores",
    )(page_table, qb, kb, *([cache_t] * PAGES_PER_STEP))

    keys = pl.BlockSpec((1, SUBLANES, n_keys), lambda b: (b, 0, 0))
    small = pl.BlockSpec((1, SUBLANES, LANES), lambda b: (b, 0, 0))
    p, pself, idx = pl.pallas_call(
        functools.partial(_mb_select_kernel, n_blocks=n_blocks),
        grid=(nseq,), in_specs=[keys, small, small], out_specs=(keys, small, small),
        out_shape=(jax.ShapeDtypeStruct((nseq, SUBLANES, n_keys), f32),
                   jax.ShapeDtypeStruct((nseq, SUBLANES, LANES), f32),
                   jax.ShapeDtypeStruct((nseq, SUBLANES, LANES), jnp.int32)),
        compiler_params=_params(1), name="moba_decode_select",
    )(z, bsum, zself)

    idx_flat = idx[:, :, :MOBA_TOPK].reshape(nseq, SUBLANES * MOBA_TOPK)

    def page_no(b, h, k, idx):
        return idx[b, h * MOBA_TOPK + k // PAGES_PER_BLOCK] * PAGES_PER_BLOCK + k % PAGES_PER_BLOCK

    v_specs = [pl.BlockSpec((None, None, None, None, HEAD_DIM, PAGE),
                            lambda b, h, pt, idx, k=k: (layer, pt[b, page_no(b, h, k, idx)], 1, h, 0, 0))
               for k in range(N_PICKED_PAGES)]
    p_specs = [pl.BlockSpec((1, SUBLANES, PAGE), lambda b, h, pt, idx, k=k: (b, 0, page_no(b, h, k, idx)))
               for k in range(N_PICKED_PAGES)]
    gs = pltpu.PrefetchScalarGridSpec(
        num_scalar_prefetch=2, grid=(nseq, n_heads),
        in_specs=v_specs + p_specs
        + [pl.BlockSpec((1, SUBLANES, LANES), lambda b, h, pt, idx: (b, 0, 0)),
           pl.BlockSpec((1, n_heads, HEAD_DIM, LANES), lambda b, h, pt, idx: (b, 0, 0, 0))],
        out_specs=pl.BlockSpec((1, SUBLANES, HEAD_DIM), lambda b, h, pt, idx: (b, 0, 0)))
    return pl.pallas_call(
        _mb_pv_kernel, grid_spec=gs, out_shape=jax.ShapeDtypeStruct((nseq, SUBLANES, HEAD_DIM), f32),
        compiler_params=_params(2), name="moba_decode_pv",
    )(page_table, idx_flat, *([cache_t] * N_PICKED_PAGES), *([p] * N_PICKED_PAGES), pself, vb)


def _rope_tables(pos):
    inv_freq = ROPE_THETA ** (-jnp.arange(ROPE_HALF, dtype=f32) / ROPE_HALF)
    ang = pos.astype(f32)[:, None] * inv_freq[None, :]
    cos, sin = jnp.cos(ang), jnp.sin(ang)
    n = pos.shape[0]
    rest = HEAD_DIM - ROPE_DIM
    cq = jnp.tile(jnp.concatenate([cos, cos, jnp.ones((n, rest), f32)], axis=1), (1, H_MB))
    sq = jnp.tile(jnp.concatenate([sin, sin, jnp.zeros((n, rest), f32)], axis=1), (1, H_MB))
    return cq, sq, cos.T, sin.T


def _hi_lo(w):
    hi = w.astype(bf16)
    return hi, (w - hi.astype(f32)).astype(bf16)


def _layer_weights(w_in, b_forget):
    scale = HEAD_DIM ** -0.5
    edges = [0]
    for wdt in (W_SB,) * 3 + (W_MB,) * 3 + (W_FX,) * 3 + (H_FX,):
        edges.append(edges[-1] + wdt)
    seg = [w_in[:, a:b] for a, b in zip(edges[:-1], edges[1:])]
    q_sb, k_sb, v_sb, q_mb, k_mb, v_mb, q_fx, k_fx, v_fx, w_f = seg
    w_g = w_in[:, edges[-1]:]
    km = k_mb.reshape(D_MODEL, H_MB, HEAD_DIM)
    k_sw = jnp.concatenate([-km[..., ROPE_HALF:ROPE_DIM], km[..., :ROPE_HALF],
                            jnp.zeros((D_MODEL, H_MB, HEAD_DIM - ROPE_DIM), f32)], axis=-1).reshape(D_MODEL, W_MB)
    wt = jnp.concatenate([q_sb * scale, q_mb * scale, q_fx * scale,
                          k_sb, v_sb, k_mb, v_mb, k_fx, v_fx], axis=1).T.astype(bf16)
    wk = jnp.concatenate([k_sb, k_mb, k_fx, k_sw], axis=1).astype(bf16)
    wf_hi, wf_lo = _hi_lo(jnp.pad(w_f, ((0, 0), (0, SUBLANES - H_FX))).T)
    wf = jnp.concatenate([wf_hi, wf_lo], axis=0)
    wfr_hi, wfr_lo = _hi_lo(jnp.pad(w_f, ((0, 0), (0, LANES - H_FX))))
    wfr = jnp.concatenate([wfr_hi, wfr_lo], axis=1)
    bfc = jnp.pad(b_forget, (0, SUBLANES - H_FX)).reshape(SUBLANES, 1)
    bfr = jnp.pad(b_forget, (0, LANES - H_FX)).reshape(1, LANES)
    return wt, wk, wf, wfr, w_g.astype(bf16), bfc, bfr


def _post_weights(w_br_sb, w_br_mb, w_br_fx, w_o, ln1_g, ln1_b, w_up, w_down, ln2_g, ln2_b):
    heads = lambda w: w.reshape(-1, HEAD_DIM, D_MODEL).astype(bf16)
    vec = lambda v: v.reshape(1, D_MODEL)
    return (heads(w_br_sb), heads(w_br_mb), heads(w_br_fx), w_o.astype(bf16), w_up.astype(bf16),
            w_down.astype(bf16), vec(ln1_g), vec(ln1_b), vec(ln2_g), vec(ln2_b))


def _lane_bcast(a):
    return jnp.broadcast_to(a[..., None], a.shape + (LANES,))


def kernel(x_prompt, x_sample, cache_kv_sb, cache_kv_moba, cache_kv_fox, cache_logf_fox, page_table,
           w_in, b_forget, w_br_sb, w_br_moba, w_br_fox, w_o, ln1_g, ln1_b, w_up, w_down, ln2_g, ln2_b):
    B, T, D = x_prompt.shape
    nseq = x_sample.shape[0]
    n_pages = page_table.shape[1]
    past_len = n_pages * PAGE

    rope_p = _rope_tables(jnp.arange(T, dtype=jnp.int32))
    rope_s = _rope_tables(jnp.full((nseq,), past_len, jnp.int32))
    sb_t, mb_t, fx_t = _cache_view(cache_kv_sb), _cache_view(cache_kv_moba), _cache_view(cache_kv_fox)
    pool = cache_logf_fox.shape[1]
    logf_t = jnp.transpose(cache_logf_fox, (0, 3, 1, 2)).reshape(DEPTH, H_FX, pool, 1, PAGE)

    yp = x_prompt
    ys = x_sample.reshape(1, nseq, D)
    rows_p, rows_s = [], []
    for l in range(DEPTH):
        w1 = _layer_weights(w_in[l], b_forget[l])
        w2 = _post_weights(w_br_sb[l], w_br_moba[l], w_br_fox[l], w_o[l], ln1_g[l], ln1_b[l],
                           w_up[l], w_down[l], ln2_g[l], ln2_b[l])

        qt, krow, kvsb, kvmb, kvfx, logf, cbc, gates = _inproj(yp, w1, rope_p, TM_INPROJ)
        o_sb = _sb_prompt(qt, krow, kvsb)
        o_mb = _moba_prompt(qt, krow, kvmb)
        o_fx = _fox_prompt(qt, krow, kvfx, cbc)
        yp = _post(yp, o_sb, o_mb, o_fx, gates, w2, TM_POST)
        rows_p.append((kvsb, kvmb, kvfx, logf))

        qt, _, kvsb, kvmb, kvfx, logf, _, gates = _inproj(ys, w1, rope_s, nseq)
        qd = _lane_bcast(jnp.transpose(qt[0], (2, 0, 1)))
        new = lambda kvt, j: _lane_bcast(jnp.transpose(kvt[0, j], (2, 0, 1)))
        lfn = _lane_bcast(logf[0].T)
        o_sb = _sb_decode(sb_t, l, page_table, qd[:, :H_SB])
        o_mb = _moba_decode(mb_t, l, page_table, qd[:, H_SB:H_SB + H_MB], new(kvmb, 0), new(kvmb, 1))
        o_fx = _fox_decode(fx_t, logf_t, l, page_table, qd[:, H_SB + H_MB:], new(kvfx, 0), new(kvfx, 1), lfn)
        heads = lambda o, H: jnp.transpose(o[:, :H], (1, 0, 2))[None].astype(bf16)
        ys = _post(ys, heads(o_sb, H_SB), heads(o_mb, H_MB), heads(o_fx, H_FX), gates, w2, nseq)
        rows_s.append((kvsb, kvmb, kvfx, logf))

    def kv_out(rows, j):
        return jnp.transpose(jnp.stack([r[j] for r in rows]), (0, 1, 5, 2, 3, 4))

    def logf_out(rows):
        return jnp.transpose(jnp.stack([r[3] for r in rows])[:, :, :H_FX], (0, 1, 3, 2))

    def sample(a):
        return jnp.swapaxes(a, 1, 2)

    return (yp, ys.reshape(nseq, 1, D),
            kv_out(rows_p, 0), kv_out(rows_p, 1), kv_out(rows_p, 2), logf_out(rows_p),
            sample(kv_out(rows_s, 0)), sample(kv_out(rows_s, 1)), sample(kv_out(rows_s, 2)),
            sample(logf_out(rows_s)))
```
